```python
import jax, jax.numpy as jnp
from jax import lax
import numpy as np

D_MODEL = 4096
BATCH = 4
SEQ = 2048
DEPTH = 1
DEC_BATCH = 128
DEC_SEQ = 8
PAST_LEN = 16384
PAGE_SIZE = 128

D_CONV = D_MODEL // 2
CONV_WIDTH = 31
HEAD_DIM_HGRN = 128
D_HGRN = D_MODEL // 2
N_HEADS_HGRN = D_HGRN // HEAD_DIM_HGRN
CHUNK = 64
N_MEM = 256
N_HEADS_X = 4
HEAD_DIM_X = 128
D_X = N_HEADS_X * HEAD_DIM_X
D_FF = 11008
FFN_CONV_WIDTH = 3
EPS = 1e-6
D_IN = 2 * D_CONV + 4 * D_HGRN + 2 * D_MODEL

kernel_name = 'hybrid_conformer_hgrn2_convffn_decode_step'


def rms_norm(x, g):
    xf = x.astype(jnp.float32)
    y = xf * lax.rsqrt(jnp.mean(xf * xf, axis=-1, keepdims=True) + EPS)
    return (y * g.astype(jnp.float32)).astype(x.dtype)


def layer_norm(x, g, b):
    xf = x.astype(jnp.float32)
    mu = jnp.mean(xf, axis=-1, keepdims=True)
    xc = xf - mu
    y = xc * lax.rsqrt(jnp.mean(xc * xc, axis=-1, keepdims=True) + EPS)
    return (y * g.astype(jnp.float32) + b.astype(jnp.float32)).astype(x.dtype)


def causal_dwconv(x_ext, w, b):
    c = x_ext.shape[-1]
    y = lax.conv_general_dilated(x_ext, w[:, None, :].astype(x_ext.dtype), window_strides=(1,),
                                 padding='VALID', dimension_numbers=('NWC', 'WIO', 'NWC'),
                                 feature_group_count=c)
    return y + b.astype(y.dtype)


def hgrn2_chunked(q, k, v, log_f, s0):
    B, T, H, DK = q.shape
    DV = v.shape[-1]
    c = min(CHUNK, T)
    n = -(-T // c)
    pad = n * c - T
    if pad:
        pw = ((0, 0), (0, pad), (0, 0), (0, 0))
        q, k, v, log_f = (jnp.pad(a, pw) for a in (q, k, v, log_f))

    def to_chunks(a):
        return a.reshape(B, n, c, H, a.shape[-1]).transpose(1, 0, 3, 2, 4)

    causal = jnp.tril(jnp.ones((c, c), dtype=bool))

    def step(S, inp):
        qi, ki, vi, gi = inp
        bcum = jnp.cumsum(gi, axis=2)
        o_inter = jnp.einsum('bhtk,bhkv->bhtv', qi * jnp.exp(bcum), S)
        rel = bcum[:, :, :, None, :] - bcum[:, :, None, :, :]
        rel = jnp.where(causal[None, None, :, :, None], rel, -jnp.inf)
        A = jnp.einsum('bhtk,bhsk,bhtsk->bhts', qi, ki, jnp.exp(rel))
        o_intra = jnp.einsum('bhts,bhsv->bhtv', A, vi)
        b_last = bcum[:, :, -1:, :]
        S_new = jnp.exp(b_last[:, :, 0, :])[..., None] * S + jnp.einsum(
            'bhsk,bhsv->bhkv', ki * jnp.exp(b_last - bcum), vi)
        return S_new, o_inter + o_intra

    S, o = lax.scan(step, s0.astype(jnp.float32),
                    (to_chunks(q), to_chunks(k), to_chunks(v), to_chunks(log_f)))
    o = o.transpose(1, 0, 3, 2, 4).reshape(B, n * c, H, DV)[:, :T]
    return o, S


def memory_kv(mem, g_mem, w_xk, w_xv):
    B = mem.shape[0]
    m = rms_norm(mem, g_mem)
    k = (m @ w_xk).reshape(B, N_MEM, N_HEADS_X, HEAD_DIM_X)
    v = (m @ w_xv).reshape(B, N_MEM, N_HEADS_X, HEAD_DIM_X)
    return k, v


def layer(x, mem_k, mem_v, conv_st, hgrn_st, ffn_st, lb,
          g_mix, w_in, w_conv_dw, b_conv_dw, ln_conv_g, ln_conv_b, w_conv_out,
          g_hgrn_norm, w_hgrn_out, w_mix_out, g_xattn, w_xq, w_xo,
          g_ffn, w_up, w_ffn_dw, b_ffn_dw, w_down):
    B, T, _ = x.shape
    H, DK = N_HEADS_HGRN, HEAD_DIM_HGRN
    f32 = jnp.float32

    h = rms_norm(x, g_mix)
    z = h @ w_in
    widths = [D_CONV, D_CONV, D_HGRN, D_HGRN, D_HGRN, D_HGRN, D_MODEL]
    splits = [int(s) for s in np.cumsum(widths)]
    za, zb, zq, zf, zi, zg, gate_a, gate_b = jnp.split(z, splits, axis=-1)

    u = za * jax.nn.sigmoid(zb)
    u_ext = jnp.concatenate([conv_st.astype(u.dtype), u], axis=1)
    ca = causal_dwconv(u_ext, w_conv_dw, b_conv_dw)
    ca = jax.nn.silu(layer_norm(ca, ln_conv_g, ln_conv_b))
    y_a = ca @ w_conv_out
    new_conv_st = u_ext[:, -(CONV_WIDTH - 1):]

    lbh = lb.reshape(H, DK).astype(f32)
    f = lbh + (1.0 - lbh) * jax.nn.sigmoid(zf.astype(f32).reshape(B, T, H, DK))
    q = jax.nn.silu(zq.astype(f32)).reshape(B, T, H, DK)
    v = zi.astype(f32).reshape(B, T, H, DK)
    o, S = hgrn2_chunked(q, 1.0 - f, v, jnp.log(f), hgrn_st)
    o = rms_norm(o, g_hgrn_norm) * jax.nn.silu(zg.astype(f32).reshape(B, T, H, DK))
    y_b = o.reshape(B, T, D_HGRN).astype(x.dtype) @ w_hgrn_out
    new_hgrn_st = S.astype(hgrn_st.dtype)

    merged = jax.nn.sigmoid(gate_a) * y_a + jax.nn.sigmoid(gate_b) * y_b
    x = x + merged @ w_mix_out

    h = rms_norm(x, g_xattn)
    qx = (h @ w_xq).reshape(B, T, N_HEADS_X, HEAD_DIM_X).astype(f32)
    s = jnp.einsum('bthd,bmhd->bhtm', qx, mem_k.astype(f32)) * (HEAD_DIM_X ** -0.5)
    p = jax.nn.softmax(s, axis=-1)
    ox = jnp.einsum('bhtm,bmhd->bthd', p, mem_v.astype(f32)).reshape(B, T, D_X).astype(x.dtype)
    x = x + ox @ w_xo

    h = rms_norm(x, g_ffn)
    up = h @ w_up
    a, val = jnp.split(up, [D_FF], axis=-1)
    a_ext = jnp.concatenate([ffn_st.astype(a.dtype), a], axis=1)
    ac = causal_dwconv(a_ext, w_ffn_dw, b_ffn_dw)
    x = x + (jax.nn.silu(ac) * val) @ w_down
    new_ffn_st = a_ext[:, -(FFN_CONV_WIDTH - 1):]
    return x, new_conv_st, new_hgrn_st, new_ffn_st


def setup_inputs(seed: int = 0) -> dict:
    key = jax.random.key(seed)
    ks = jax.random.split(key, 40)
    f32 = jnp.float32

    def nrm(k, shape, scale):
        return jax.random.normal(k, shape, f32) * scale

    def gain(k, shape):
        return 1.0 + 0.1 * jax.random.normal(k, shape, f32)

    L = DEPTH
    return {
        'x_prompt': nrm(ks[0], (BATCH, SEQ, D_MODEL), 1.0),
        'x_sample': nrm(ks[1], (DEC_BATCH, DEC_SEQ, D_MODEL), 1.0),
        'mem_prompt': nrm(ks[2], (BATCH, N_MEM, D_MODEL), 1.0),
        'state_conv': nrm(ks[3], (L, DEC_BATCH, CONV_WIDTH - 1, D_CONV), 0.5),
        'state_hgrn': nrm(ks[4], (L, DEC_BATCH, N_HEADS_HGRN, HEAD_DIM_HGRN, HEAD_DIM_HGRN), 0.5),
        'cache_mem_k': nrm(ks[5], (L, DEC_BATCH, N_MEM, N_HEADS_X, HEAD_DIM_X), 1.0),
        'cache_mem_v': nrm(ks[6], (L, DEC_BATCH, N_MEM, N_HEADS_X, HEAD_DIM_X), 1.0),
        'state_ffn_conv': nrm(ks[7], (L, DEC_BATCH, FFN_CONV_WIDTH - 1, D_FF), 1.0),
        'g_mix': gain(ks[8], (L, D_MODEL)),
        'w_in': nrm(ks[9], (L, D_MODEL, D_IN), D_MODEL ** -0.5),
        'w_conv_dw': nrm(ks[10], (L, CONV_WIDTH, D_CONV), CONV_WIDTH ** -0.5),
        'b_conv_dw': nrm(ks[11], (L, D_CONV), 0.02),
        'ln_conv_g': gain(ks[12], (L, D_CONV)),
        'ln_conv_b': nrm(ks[13], (L, D_CONV), 0.02),
        'w_conv_out': nrm(ks[14], (L, D_CONV, D_MODEL), D_CONV ** -0.5),
        'lb_logits': nrm(ks[15], (L + 1, D_HGRN), 0.5),
        'g_hgrn_norm': gain(ks[16], (L, HEAD_DIM_HGRN)),
        'w_hgrn_out': nrm(ks[17], (L, D_HGRN, D_MODEL), D_HGRN ** -0.5),
        'w_mix_out': nrm(ks[18], (L, D_MODEL, D_MODEL), D_MODEL ** -0.5),
        'g_xattn': gain(ks[19], (L, D_MODEL)),
        'g_mem': gain(ks[20], (L, D_MODEL)),
        'w_xq': nrm(ks[21], (L, D_MODEL, D_X), D_MODEL ** -0.5),
        'w_xk': nrm(ks[22], (L, D_MODEL, D_X), D_MODEL ** -0.5),
        'w_xv': nrm(ks[23], (L, D_MODEL, D_X), D_MODEL ** -0.5),
        'w_xo': nrm(ks[24], (L, D_X, D_MODEL), D_X ** -0.5),
        'g_ffn': gain(ks[25], (L, D_MODEL)),
        'w_up': nrm(ks[26], (L, D_MODEL, 2 * D_FF), D_MODEL ** -0.5),
        'w_ffn_dw': nrm(ks[27], (L, FFN_CONV_WIDTH, D_FF), FFN_CONV_WIDTH ** -0.5),
        'b_ffn_dw': nrm(ks[28], (L, D_FF), 0.02),
        'w_down': nrm(ks[29], (L, D_FF, D_MODEL), D_FF ** -0.5),
        'g_final': gain(ks[30], (D_MODEL,)),
    }


def reference(x_prompt, x_sample, mem_prompt, state_conv, state_hgrn, cache_mem_k, cache_mem_v,
              state_ffn_conv, g_mix, w_in, w_conv_dw, b_conv_dw, ln_conv_g, ln_conv_b, w_conv_out,
              lb_logits, g_hgrn_norm, w_hgrn_out, w_mix_out, g_xattn, g_mem, w_xq, w_xk, w_xv, w_xo,
              g_ffn, w_up, w_ffn_dw, b_ffn_dw, w_down, g_final):
    lb_all = jnp.cumsum(jax.nn.softmax(lb_logits.astype(jnp.float32), axis=0), axis=0)[:DEPTH]
    xp, xs = x_prompt, x_sample
    p_conv, p_hgrn, p_mk, p_mv, p_ffn = [], [], [], [], []
    s_conv, s_hgrn, s_ffn = [], [], []
    for l in range(DEPTH):
        w = (g_mix[l], w_in[l], w_conv_dw[l], b_conv_dw[l], ln_conv_g[l], ln_conv_b[l], w_conv_out[l],
             g_hgrn_norm[l], w_hgrn_out[l], w_mix_out[l], g_xattn[l], w_xq[l], w_xo[l],
             g_ffn[l], w_up[l], w_ffn_dw[l], b_ffn_dw[l], w_down[l])
        mk, mv = memory_kv(mem_prompt, g_mem[l], w_xk[l], w_xv[l])
        zc = jnp.zeros((BATCH, CONV_WIDTH - 1, D_CONV), xp.dtype)
        zh = jnp.zeros((BATCH, N_HEADS_HGRN, HEAD_DIM_HGRN, HEAD_DIM_HGRN), xp.dtype)
        zf = jnp.zeros((BATCH, FFN_CONV_WIDTH - 1, D_FF), xp.dtype)
        xp, c1, h1, f1 = layer(xp, mk, mv, zc, zh, zf, lb_all[l], *w)
        p_conv.append(c1); p_hgrn.append(h1); p_mk.append(mk); p_mv.append(mv); p_ffn.append(f1)
        xs, c2, h2, f2 = layer(xs, cache_mem_k[l], cache_mem_v[l], state_conv[l], state_hgrn[l],
                               state_ffn_conv[l], lb_all[l], *w)
        s_conv.append(c2); s_hgrn.append(h2); s_ffn.append(f2)
    y_prompt = rms_norm(xp, g_final)
    y_sample = rms_norm(xs, g_final)
    return (y_prompt, y_sample,
            jnp.stack(p_conv), jnp.stack(p_hgrn), jnp.stack(p_mk), jnp.stack(p_mv), jnp.stack(p_ffn),
            jnp.stack(s_conv), jnp.stack(s_hgrn), jnp.stack(s_ffn))
```

```python
import functools
import math

import jax
import jax.numpy as jnp
import numpy as np
from jax import lax
from jax.experimental import pallas as pl
from jax.experimental.pallas import tpu as pltpu

F32 = jnp.float32
BF16 = jnp.bfloat16
EPS = 1e-6

LANES = 128
SUBLANES = 8
BF16_SUBLANES = 16
VMEM_LIMIT_BYTES = 56 * 1024 * 1024

HGRN_HEAD_DIM = 128
XATTN_HEAD_DIM = 128
HGRN_CHUNK = 64
HGRN_ROWS = 1024
FFN_TM = 1024


def _pick(n, target, mult):
    if n <= target:
        return n
    best = None
    for d in range(mult, target + 1, mult):
        if n % d == 0:
            best = d
    assert best is not None, (n, target, mult)
    return best


def _params(*sem):
    return pltpu.CompilerParams(dimension_semantics=sem, vmem_limit_bytes=VMEM_LIMIT_BYTES)


def _sigmoid(x):
    return 1.0 / (1.0 + jnp.exp(-x))


def _dot(a, b):
    return jnp.dot(a, b, preferred_element_type=F32)


def _dot_nt(a, b):
    return lax.dot_general(a, b, (((1,), (1,)), ((), ())), preferred_element_type=F32)


def _dot_tn(a, b):
    return lax.dot_general(a, b, (((0,), (0,)), ((), ())), preferred_element_type=F32)


def _rmsnorm_body(x_ref, g_ref, o_ref):
    x = x_ref[...]
    ms = jnp.mean(x * x, axis=-1, keepdims=True)
    o_ref[...] = (x * lax.rsqrt(ms + EPS) * g_ref[...]).astype(o_ref.dtype)


def _rmsnorm(x, g, out_dtype):
    m, d = x.shape
    tr = _pick(m, 256, BF16_SUBLANES)
    return pl.pallas_call(
        _rmsnorm_body,
        out_shape=jax.ShapeDtypeStruct((m, d), out_dtype),
        grid=(m // tr,),
        in_specs=[pl.BlockSpec((tr, d), lambda i: (i, 0)),
                  pl.BlockSpec((1, d), lambda i: (0, 0))],
        out_specs=pl.BlockSpec((tr, d), lambda i: (i, 0)),
        compiler_params=_params("parallel"),
        name="rmsnorm",
    )(x, g.reshape(1, d))


def _mm_body(a_ref, w_ref, o_ref):
    o_ref[...] = _dot(a_ref[...], w_ref[...]).astype(o_ref.dtype)


def _mm_res_body(a_ref, w_ref, r_ref, o_ref):
    o_ref[...] = (r_ref[...] + _dot(a_ref[...], w_ref[...])).astype(o_ref.dtype)


def _matmul(a, w, res=None, out_dtype=F32, tm=1024, tn=1024, name="matmul"):
    m, k = a.shape
    n = w.shape[1]
    tm = _pick(m, tm, BF16_SUBLANES)
    tn = _pick(n, tn, LANES)
    in_specs = [pl.BlockSpec((tm, k), lambda i, j: (i, 0)),
                pl.BlockSpec((k, tn), lambda i, j: (0, j))]
    args = [a, w]
    body = _mm_body
    if res is not None:
        in_specs.append(pl.BlockSpec((tm, tn), lambda i, j: (i, j)))
        args.append(res)
        body = _mm_res_body
    return pl.pallas_call(
        body,
        out_shape=jax.ShapeDtypeStruct((m, n), out_dtype),
        grid=(m // tm, n // tn),
        in_specs=in_specs,
        out_specs=pl.BlockSpec((tm, tn), lambda i, j: (i, j)),
        compiler_params=_params("parallel", "parallel"),
        name=name,
    )(*args)


CONV_HALO = 32


def _conv_body(za_ref, zb_ref, st_ref, w_ref, b_ref, lg_ref, lb_ref, ca_ref, nst_ref, uext, cbuf,
               *, nb, tt, c, width, cch, rch):
    ti = pl.program_id(1)
    h0 = CONV_HALO - (width - 1)
    for s in range(nb):
        @pl.when(ti == 0)
        def _():
            uext[s, h0:CONV_HALO, :] = st_ref[s]

        @pl.when(ti > 0)
        def _():
            uext[s, h0:CONV_HALO, :] = uext[s, h0 + tt:CONV_HALO + tt, :]

        uext[s, CONV_HALO:CONV_HALO + tt, :] = za_ref[s] * _sigmoid(zb_ref[s])
        for c0 in range(0, c, cch):
            for r0 in range(0, tt, rch):
                acc = jnp.broadcast_to(b_ref[:, c0:c0 + cch], (rch, cch))
                for k in range(width):
                    lo = h0 + r0 + k
                    acc = acc + uext[s, lo:lo + rch, c0:c0 + cch] * w_ref[k:k + 1, c0:c0 + cch]
                cbuf[s * tt + r0:s * tt + r0 + rch, c0:c0 + cch] = acc
        nst_ref[s] = uext[s, CONV_HALO + tt - (width - 1):CONV_HALO + tt, :]
    x = cbuf[...]
    mu = jnp.mean(x, axis=-1, keepdims=True)
    xc = x - mu
    var = jnp.mean(xc * xc, axis=-1, keepdims=True)
    y = xc * lax.rsqrt(var + EPS) * lg_ref[...] + lb_ref[...]
    ca_ref[...] = (y * _sigmoid(y)).astype(ca_ref.dtype)


def _conformer_conv(z, state, w_dw, b_dw, ln_g, ln_b, bsz, t, c):
    width = w_dw.shape[0]
    d_in = z.shape[1]
    z3 = z.reshape(bsz, t, d_in)
    if t >= 128:
        nb, tt = 1, 128
    else:
        nb, tt = _pick(bsz, max(1, 128 // t), 1), t
    assert t % tt == 0 and bsz % nb == 0 and (tt >= width - 1 or tt == t)
    assert (nb * tt) % BF16_SUBLANES == 0
    cch = _pick(c, 512, LANES)
    rch = _pick(tt, 64, SUBLANES)
    body = functools.partial(_conv_body, nb=nb, tt=tt, c=c, width=width, cch=cch, rch=rch)
    nt = t // tt
    ca, nst = pl.pallas_call(
        body,
        out_shape=(jax.ShapeDtypeStruct((bsz * t, c), BF16),
                   jax.ShapeDtypeStruct((bsz, width - 1, c), F32)),
        grid=(bsz // nb, nt),
        in_specs=[pl.BlockSpec((nb, tt, c), lambda b, i: (b, i, 0)),
                  pl.BlockSpec((nb, tt, c), lambda b, i: (b, i, 1)),
                  pl.BlockSpec((nb, width - 1, c), lambda b, i: (b, 0, 0)),
                  pl.BlockSpec((width, c), lambda b, i: (0, 0)),
                  pl.BlockSpec((1, c), lambda b, i: (0, 0)),
                  pl.BlockSpec((1, c), lambda b, i: (0, 0)),
                  pl.BlockSpec((1, c), lambda b, i: (0, 0))],
        out_specs=(pl.BlockSpec((nb * tt, c), lambda b, i: (b * nt + i, 0)),
                   pl.BlockSpec((nb, width - 1, c), lambda b, i: (b, 0, 0))),
        scratch_shapes=[pltpu.VMEM((nb, CONV_HALO + tt, c), F32),
                        pltpu.VMEM((nb * tt, c), F32)],
        compiler_params=_params("parallel", "arbitrary"),
        name="conformer_conv",
    )(z3, z3, state, w_dw, b_dw.reshape(1, c), ln_g.reshape(1, c), ln_b.reshape(1, c))
    return ca, nst


def _hgrn_consts(c):
    nlev = int(math.log2(c))
    assert 1 << nlev == c
    t = np.arange(c)
    mats = np.zeros((nlev + 1, c, c), np.float32)
    mats[0] = (t[None, :] <= t[:, None])
    masks = np.zeros((nlev, c, c), np.float32)
    for lev in range(1, nlev + 1):
        half = 1 << (lev - 1)
        blk = t >> lev
        upper = ((t >> (lev - 1)) & 1).astype(bool)
        p = (blk << lev) + half - 1
        r = t[None, :]
        up_rows = (r > p[:, None]) & (r <= t[:, None])
        lo_rows = (r > t[:, None]) & (r <= p[:, None])
        mats[lev] = np.where(upper[:, None], up_rows, lo_rows)
        masks[lev - 1] = (blk[:, None] == blk[None, :]) & upper[:, None] & (~upper[None, :])
    return mats.reshape((nlev + 1) * c, c), masks


def _hgrn_body(zq_ref, zf_ref, zi_ref, zg_ref, lbl_ref, gn_ref, mall_ref, masks_ref, s0_ref,
               o_ref, sout_ref, s_scr, *, nseq, n_chunks, c, nlev, mxu_dtype):
    ti = pl.program_id(2)
    dk = HGRN_HEAD_DIM

    @pl.when(ti == 0)
    def _():
        s_scr[...] = s0_ref[:, 0]

    l0 = lbl_ref[0:1, :]
    l1 = lbl_ref[1:2, :]
    lmax = jnp.maximum(l0, l1)
    e0 = jnp.exp(l0 - lmax)
    lb = e0 / (e0 + jnp.exp(l1 - lmax))
    gn = gn_ref[...]
    mall = mall_ref[...]
    ones = jnp.ones((c, dk), mxu_dtype)

    def chunk(r0, s_prev):
        zq = zq_ref[pl.ds(r0, c), :]
        zf = zf_ref[pl.ds(r0, c), :]
        v = zi_ref[pl.ds(r0, c), :]
        zg = zg_ref[pl.ds(r0, c), :]
        f = lb + (1.0 - lb) * _sigmoid(zf)
        g = jnp.log(f)
        kk = 1.0 - f
        q = zq * _sigmoid(zq)
        g1 = g.astype(BF16).astype(F32)
        r1 = g - g1
        g2 = r1.astype(BF16).astype(F32)
        g3 = (r1 - g2).astype(BF16).astype(F32)
        g_parts = jnp.concatenate([g1, g2, g3], axis=1).astype(mxu_dtype)
        e3 = _dot(mall, g_parts)
        e_all = e3[:, 0:dk] + e3[:, dk:2 * dk] + e3[:, 2 * dk:3 * dk]
        bcum = e_all[0:c]
        bl3 = _dot_tn(g_parts, ones)
        bl_col = bl3[0:dk] + bl3[dk:2 * dk] + bl3[2 * dk:3 * dk]
        vm = v.astype(mxu_dtype)
        a = jnp.zeros((c, c), F32)
        for lev in range(1, nlev + 1):
            e = jnp.exp(e_all[lev * c:(lev + 1) * c])
            p = _dot_nt((q * e).astype(mxu_dtype), (kk * e).astype(mxu_dtype))
            a = a + masks_ref[lev - 1] * p
        d = jnp.sum(q * kk, axis=-1, keepdims=True)
        o = (_dot((q * jnp.exp(bcum)).astype(mxu_dtype), s_prev.astype(mxu_dtype))
             + _dot(a.astype(mxu_dtype), vm) + d * v)
        b_last = bcum[c - 1:c, :]
        kd = kk * jnp.exp(b_last - bcum)
        s_new = jnp.exp(bl_col) * s_prev + _dot_tn(kd.astype(mxu_dtype), vm)
        ms = jnp.mean(o * o, axis=-1, keepdims=True)
        o_ref[pl.ds(r0, c), :] = o * lax.rsqrt(ms + EPS) * gn * (zg * _sigmoid(zg))
        return s_new

    def seq_loop(s, carry):
        def chunk_loop(n, s_prev):
            r0 = pl.multiple_of((s * n_chunks + n) * c, c)
            return chunk(r0, s_prev)
        s_fin = lax.fori_loop(0, n_chunks, chunk_loop, s_scr[s])
        s_scr[s] = s_fin
        return carry

    lax.fori_loop(0, nseq, seq_loop, 0)

    @pl.when(ti == pl.num_programs(2) - 1)
    def _():
        sout_ref[:, 0] = s_scr[...]


def _hgrn(z, lb_logits, g_norm, state, bsz, t, d_hgrn, col0):
    dk = HGRN_HEAD_DIM
    nh = d_hgrn // dk
    c = min(HGRN_CHUNK, t)
    assert t % c == 0 and col0 % dk == 0
    if t >= 512:
        nseq, rt = 1, _pick(t, HGRN_ROWS, c)
    else:
        nseq, rt = _pick(bsz, max(1, 256 // t), 1), t
    n_chunks = rt // c
    nlev = int(math.log2(c))
    mxu_dtype = BF16 if c % BF16_SUBLANES == 0 else F32
    mats, masks = _hgrn_consts(c)
    rows = nseq * rt
    nt = t // rt
    cb = col0 // dk

    def zspec(k):
        return pl.BlockSpec((rows, dk), lambda b, h, i: (b * nt + i, cb + k * nh + h))

    body = functools.partial(_hgrn_body, nseq=nseq, n_chunks=n_chunks, c=c, nlev=nlev, mxu_dtype=mxu_dtype)
    o, s_out = pl.pallas_call(
        body,
        out_shape=(jax.ShapeDtypeStruct((bsz * t, d_hgrn), F32),
                   jax.ShapeDtypeStruct((bsz, nh, dk, dk), F32)),
        grid=(bsz // nseq, nh, nt),
        in_specs=[zspec(0), zspec(1), zspec(2), zspec(3),
                  pl.BlockSpec((2, dk), lambda b, h, i: (0, h)),
                  pl.BlockSpec((1, dk), lambda b, h, i: (0, 0)),
                  pl.BlockSpec(mats.shape, lambda b, h, i: (0, 0)),
                  pl.BlockSpec(masks.shape, lambda b, h, i: (0, 0, 0)),
                  pl.BlockSpec((nseq, 1, dk, dk), lambda b, h, i: (b, h, 0, 0))],
        out_specs=(pl.BlockSpec((rows, dk), lambda b, h, i: (b * nt + i, h)),
                   pl.BlockSpec((nseq, 1, dk, dk), lambda b, h, i: (b, h, 0, 0))),
        scratch_shapes=[pltpu.VMEM((nseq, dk, dk), F32)],
        compiler_params=_params("parallel", "parallel", "arbitrary"),
        name="hgrn2",
    )(z, z, z, z, lb_logits, g_norm.reshape(1, dk), jnp.asarray(mats, mxu_dtype), jnp.asarray(masks), state)
    return o, s_out


def _merge_body(ca_ref, ob_ref, wc_ref, wh_ref, ga_ref, gb_ref, o_ref):
    ya = _dot(ca_ref[...], wc_ref[...])
    yb = _dot(ob_ref[...].astype(BF16), wh_ref[...])
    o_ref[...] = (_sigmoid(ga_ref[...]) * ya + _sigmoid(gb_ref[...]) * yb).astype(o_ref.dtype)


def _merge(ca, ob, w_conv_out, w_hgrn_out, z, gate_col0, d_model):
    m, kc = ca.shape
    kh = ob.shape[1]
    tm = _pick(m, 512, BF16_SUBLANES)
    tn = _pick(d_model, 512, LANES)
    assert gate_col0 % tn == 0
    ga0 = gate_col0 // tn
    gb0 = (gate_col0 + d_model) // tn
    return pl.pallas_call(
        _merge_body,
        out_shape=jax.ShapeDtypeStruct((m, d_model), BF16),
        grid=(m // tm, d_model // tn),
        in_specs=[pl.BlockSpec((tm, kc), lambda i, j: (i, 0)),
                  pl.BlockSpec((tm, kh), lambda i, j: (i, 0)),
                  pl.BlockSpec((kc, tn), lambda i, j: (0, j)),
                  pl.BlockSpec((kh, tn), lambda i, j: (0, j)),
                  pl.BlockSpec((tm, tn), lambda i, j: (i, ga0 + j)),
                  pl.BlockSpec((tm, tn), lambda i, j: (i, gb0 + j))],
        out_specs=pl.BlockSpec((tm, tn), lambda i, j: (i, j)),
        compiler_params=_params("parallel", "parallel"),
        name="mixer_merge",
    )(ca, ob, w_conv_out, w_hgrn_out, z, z)


def _xattn_body(q_ref, k_ref, v_ref, o_ref, *, nb, tq, nh, mxu_dtype):
    dh = XATTN_HEAD_DIM
    scale = dh ** -0.5
    rows = []
    for s in range(nb):
        q = q_ref[s]
        kmat = k_ref[s].astype(mxu_dtype)
        vmat = v_ref[s].astype(mxu_dtype)
        heads = []
        for h in range(nh):
            sl = slice(h * dh, (h + 1) * dh)
            sc = _dot_nt(q[:, sl].astype(mxu_dtype), kmat[:, sl]) * scale
            mx = jnp.max(sc, axis=-1, keepdims=True)
            p = jnp.exp(sc - mx)
            p = p / jnp.sum(p, axis=-1, keepdims=True)
            heads.append(_dot(p.astype(mxu_dtype), vmat[:, sl]))
        rows.append(jnp.concatenate(heads, axis=1))
    out = rows[0] if nb == 1 else jnp.concatenate(rows, axis=0)
    o_ref[...] = out.astype(o_ref.dtype)


def _xattn(qx, mem_k, mem_v, bsz, t):
    dx = qx.shape[1]
    n_mem = mem_k.shape[1]
    nh = dx // XATTN_HEAD_DIM
    if t >= 16:
        nb, tq = 1, _pick(t, 512, BF16_SUBLANES)
    else:
        nb, tq = _pick(bsz, max(1, 64 // t), 1), t
    assert (nb * tq) % BF16_SUBLANES == 0
    mxu_dtype = BF16 if tq % BF16_SUBLANES == 0 else F32
    nt = t // tq
    q3 = qx.reshape(bsz, t, dx)
    body = functools.partial(_xattn_body, nb=nb, tq=tq, nh=nh, mxu_dtype=mxu_dtype)
    return pl.pallas_call(
        body,
        out_shape=jax.ShapeDtypeStruct((bsz * t, dx), BF16),
        grid=(bsz // nb, nt),
        in_specs=[pl.BlockSpec((nb, tq, dx), lambda b, i: (b, i, 0)),
                  pl.BlockSpec((nb, n_mem, dx), lambda b, i: (b, 0, 0)),
                  pl.BlockSpec((nb, n_mem, dx), lambda b, i: (b, 0, 0))],
        out_specs=pl.BlockSpec((nb * tq, dx), lambda b, i: (b * nt + i, 0)),
        compiler_params=_params("parallel", "parallel"),
        name="xattn",
    )(q3, mem_k, mem_v)


def _ffn_up_body(h_ref, wa_ref, wv_ref, dw_ref, db_ref, st_ref, o_ref, nst_ref, carry,
                 *, nseq, rows_per_seq, pieces):
    i = pl.program_id(0)
    j = pl.program_id(1)
    h = h_ref[...]
    a = _dot(h, wa_ref[...])
    val = _dot(h, wv_ref[...])
    tm, tn = a.shape
    if pieces == 1:
        prev = st_ref[...]
    else:
        prev = jnp.where(i % pieces == 0, st_ref[...], carry[j][None])
    carry[j] = a[tm - 2:tm, :]
    p0 = jnp.broadcast_to(prev[:, 0:1, :], (nseq, rows_per_seq, tn)).reshape(tm, tn)
    p1 = jnp.broadcast_to(prev[:, 1:2, :], (nseq, rows_per_seq, tn)).reshape(tm, tn)
    tpos = lax.broadcasted_iota(jnp.int32, (tm, tn), 0) % rows_per_seq
    a1 = jnp.where(tpos == 0, p1, pltpu.roll(a, 1, 0))
    a2 = jnp.where(tpos == 0, p0, jnp.where(tpos == 1, p1, pltpu.roll(a, 2, 0)))
    ac = dw_ref[0:1, :] * a2 + dw_ref[1:2, :] * a1 + dw_ref[2:3, :] * a + db_ref[...]
    o_ref[...] = (ac * _sigmoid(ac) * val).astype(o_ref.dtype)
    nst_ref[...] = a.reshape(nseq, rows_per_seq, tn)[:, rows_per_seq - 2:rows_per_seq, :]


def _ffn_up(h, w_up, w_dw, b_dw, state, bsz, t, d_ff):
    m, d = h.shape
    tm = _pick(m, FFN_TM, BF16_SUBLANES)
    tn = _pick(d_ff, 256, LANES)
    if t >= tm:
        assert t % tm == 0
        nseq, rows_per_seq, pieces = 1, tm, t // tm
    else:
        assert tm % t == 0 and tm == m
        nseq, rows_per_seq, pieces = tm // t, t, 1
    nj = d_ff // tn
    body = functools.partial(_ffn_up_body, nseq=nseq, rows_per_seq=rows_per_seq, pieces=pieces)

    def seq_idx(i):
        return i // pieces if nseq == 1 else i

    return pl.pallas_call(
        body,
        out_shape=(jax.ShapeDtypeStruct((m, d_ff), BF16),
                   jax.ShapeDtypeStruct((bsz, 2, d_ff), F32)),
        grid=(m // tm, nj),
        in_specs=[pl.BlockSpec((tm, d), lambda i, j: (i, 0)),
                  pl.BlockSpec((d, tn), lambda i, j: (0, j)),
                  pl.BlockSpec((d, tn), lambda i, j: (0, nj + j)),
                  pl.BlockSpec((3, tn), lambda i, j: (0, j)),
                  pl.BlockSpec((1, tn), lambda i, j: (0, j)),
                  pl.BlockSpec((nseq, 2, tn), lambda i, j: (seq_idx(i), 0, j))],
        out_specs=(pl.BlockSpec((tm, tn), lambda i, j: (i, j)),
                   pl.BlockSpec((nseq, 2, tn), lambda i, j: (seq_idx(i), 0, j))),
        scratch_shapes=[pltpu.VMEM((nj, 2, tn), F32)],
        compiler_params=_params("arbitrary", "arbitrary"),
        name="ffn_up",
    )(h, w_up, w_up, w_dw, b_dw.reshape(1, d_ff), state)


def _layer(x, mem_k, mem_v, conv_st, hgrn_st, ffn_st, lb_logits, wts, bsz, t):
    d_model = x.shape[1]
    d_conv = wts["w_conv_out"].shape[0]
    d_hgrn = wts["w_hgrn_out"].shape[0]
    d_ff = wts["w_down"].shape[0]

    h = _rmsnorm(x, wts["g_mix"], BF16)
    z = _matmul(h, wts["w_in"], name="in_proj")
    ca, new_conv = _conformer_conv(z, conv_st, wts["w_conv_dw"], wts["b_conv_dw"],
                                   wts["ln_conv_g"], wts["ln_conv_b"], bsz, t, d_conv)
    ob, new_hgrn = _hgrn(z, lb_logits, wts["g_hgrn_norm"], hgrn_st, bsz, t, d_hgrn, 2 * d_conv)
    merged = _merge(ca, ob, wts["w_conv_out"], wts["w_hgrn_out"], z, 2 * d_conv + 4 * d_hgrn, d_model)
    x = _matmul(merged, wts["w_mix_out"], res=x, name="mix_out")

    h = _rmsnorm(x, wts["g_xattn"], BF16)
    qx = _matmul(h, wts["w_xq"], name="xattn_q")
    ox = _xattn(qx, mem_k, mem_v, bsz, t)
    x = _matmul(ox, wts["w_xo"], res=x, name="xattn_out")

    h = _rmsnorm(x, wts["g_ffn"], BF16)
    hmid, new_ffn = _ffn_up(h, wts["w_up"], wts["w_ffn_dw"], wts["b_ffn_dw"], ffn_st, bsz, t, d_ff)
    x = _matmul(hmid, wts["w_down"], res=x, tm=512, tn=256, name="ffn_down")
    return x, new_conv, new_hgrn, new_ffn


def kernel(x_prompt, x_sample, mem_prompt, state_conv, state_hgrn, cache_mem_k, cache_mem_v, state_ffn_conv, g_mix, w_in, w_conv_dw, b_conv_dw, ln_conv_g, ln_conv_b, w_conv_out, lb_logits, g_hgrn_norm, w_hgrn_out, w_mix_out, g_xattn, g_mem, w_xq, w_xk, w_xv, w_xo, g_ffn, w_up, w_ffn_dw, b_ffn_dw, w_down, g_final):
    depth = w_in.shape[0]
    assert depth == 1 and lb_logits.shape[0] == depth + 1
    bp, tp, d_model = x_prompt.shape
    bs, ts, _ = x_sample.shape
    n_mem = mem_prompt.shape[1]
    d_x = w_xq.shape[2]
    nhx = d_x // XATTN_HEAD_DIM
    d_conv = w_conv_out.shape[1]
    d_hgrn = w_hgrn_out.shape[1]
    d_ff = w_down.shape[1]
    nh = d_hgrn // HGRN_HEAD_DIM
    cw = w_conv_dw.shape[1]
    l = 0

    wts = dict(
        g_mix=g_mix[l], w_in=w_in[l].astype(BF16), w_conv_dw=w_conv_dw[l], b_conv_dw=b_conv_dw[l],
        ln_conv_g=ln_conv_g[l], ln_conv_b=ln_conv_b[l], w_conv_out=w_conv_out[l].astype(BF16),
        g_hgrn_norm=g_hgrn_norm[l], w_hgrn_out=w_hgrn_out[l].astype(BF16),
        w_mix_out=w_mix_out[l].astype(BF16), g_xattn=g_xattn[l], w_xq=w_xq[l].astype(BF16),
        w_xo=w_xo[l].astype(BF16), g_ffn=g_ffn[l], w_up=w_up[l].astype(BF16),
        w_ffn_dw=w_ffn_dw[l], b_ffn_dw=b_ffn_dw[l], w_down=w_down[l].astype(BF16))

    mem_h = _rmsnorm(mem_prompt.reshape(bp * n_mem, d_model), g_mem[l], BF16)
    mk = _matmul(mem_h, w_xk[l].astype(BF16), name="mem_k")
    mv = _matmul(mem_h, w_xv[l].astype(BF16), name="mem_v")
    xp, p_conv, p_hgrn, p_ffn = _layer(
        x_prompt.reshape(bp * tp, d_model), mk.reshape(bp, n_mem, d_x), mv.reshape(bp, n_mem, d_x),
        jnp.zeros((bp, cw - 1, d_conv), F32), jnp.zeros((bp, nh, HGRN_HEAD_DIM, HGRN_HEAD_DIM), F32),
        jnp.zeros((bp, 2, d_ff), F32), lb_logits, wts, bp, tp)
    xs, s_conv, s_hgrn, s_ffn = _layer(
        x_sample.reshape(bs * ts, d_model), cache_mem_k[l].reshape(bs, n_mem, d_x),
        cache_mem_v[l].reshape(bs, n_mem, d_x), state_conv[l], state_hgrn[l], state_ffn_conv[l],
        lb_logits, wts, bs, ts)

    y_prompt = _rmsnorm(xp, g_final, F32).reshape(bp, tp, d_model)
    y_sample = _rmsnorm(xs, g_final, F32).reshape(bs, ts, d_model)
    return (y_prompt, y_sample,
            p_conv[None], p_hgrn[None],
            mk.reshape(1, bp, n_mem, nhx, XATTN_HEAD_DIM), mv.reshape(1, bp, n_mem, nhx, XATTN_HEAD_DIM),
            p_ffn[None],
            s_conv[None], s_hgrn[None], s_ffn[None])
```

```python
import functools
import math

import jax
import jax.numpy as jnp
import numpy as np
from jax import lax
from jax.experimental import pallas as pl
from jax.experimental.pallas import tpu as pltpu

F32 = jnp.float32
BF16 = jnp.bfloat16
EPS = 1e-6

LANES = 128
SUBLANES = 8
BF16_SUBLANES = 16
VMEM_LIMIT_BYTES = 56 * 1024 * 1024

HGRN_HEAD_DIM = 128
XATTN_HEAD_DIM = 128
HGRN_CHUNK = 128
HGRN_HEADS_PER_STEP = 4
HGRN_ROWS = 1024
FFN_TM = 1024


def _pick(n, target, mult):
    if n <= target:
        return n
    best = None
    for d in range(mult, target + 1, mult):
        if n % d == 0:
            best = d
    assert best is not None, (n, target, mult)
    return best


def _params(*sem):
    return pltpu.CompilerParams(dimension_semantics=sem, vmem_limit_bytes=VMEM_LIMIT_BYTES)


def _sigmoid(x):
    return 1.0 / (1.0 + jnp.exp(-x))


def _dot(a, b):
    return jnp.dot(a, b, preferred_element_type=F32)


def _dot_nt(a, b):
    return lax.dot_general(a, b, (((1,), (1,)), ((), ())), preferred_element_type=F32)


def _dot_tn(a, b):
    return lax.dot_general(a, b, (((0,), (0,)), ((), ())), preferred_element_type=F32)


def _rmsnorm_body(x_ref, g_ref, o_ref):
    x = x_ref[...]
    ms = jnp.mean(x * x, axis=-1, keepdims=True)
    o_ref[...] = (x * lax.rsqrt(ms + EPS) * g_ref[...]).astype(o_ref.dtype)


def _rmsnorm(x, g, out_dtype):
    m, d = x.shape
    tr = _pick(m, 256, BF16_SUBLANES)
    return pl.pallas_call(
        _rmsnorm_body,
        out_shape=jax.ShapeDtypeStruct((m, d), out_dtype),
        grid=(m // tr,),
        in_specs=[pl.BlockSpec((tr, d), lambda i: (i, 0)),
                  pl.BlockSpec((1, d), lambda i: (0, 0))],
        out_specs=pl.BlockSpec((tr, d), lambda i: (i, 0)),
        compiler_params=_params("parallel"),
        name="rmsnorm",
    )(x, g.reshape(1, d))


def _mm_body(a_ref, w_ref, o_ref):
    o_ref[...] = _dot(a_ref[...], w_ref[...]).astype(o_ref.dtype)


def _mm_res_body(a_ref, w_ref, r_ref, o_ref):
    o_ref[...] = (r_ref[...] + _dot(a_ref[...], w_ref[...])).astype(o_ref.dtype)


def _matmul(a, w, res=None, out_dtype=F32, tm=1024, tn=1024, name="matmul"):
    m, k = a.shape
    n = w.shape[1]
    tm = _pick(m, tm, BF16_SUBLANES)
    tn = _pick(n, tn, LANES)
    in_specs = [pl.BlockSpec((tm, k), lambda i, j: (i, 0)),
                pl.BlockSpec((k, tn), lambda i, j: (0, j))]
    args = [a, w]
    body = _mm_body
    if res is not None:
        in_specs.append(pl.BlockSpec((tm, tn), lambda i, j: (i, j)))
        args.append(res)
        body = _mm_res_body
    return pl.pallas_call(
        body,
        out_shape=jax.ShapeDtypeStruct((m, n), out_dtype),
        grid=(m // tm, n // tn),
        in_specs=in_specs,
        out_specs=pl.BlockSpec((tm, tn), lambda i, j: (i, j)),
        compiler_params=_params("parallel", "parallel"),
        name=name,
    )(*args)


CONV_HALO = 32


def _conv_body(za_ref, zb_ref, st_ref, w_ref, b_ref, lg_ref, lb_ref, ca_ref, nst_ref, uext, ushift, cbuf,
               *, nb, tt, c, width, cch, rch):
    ti = pl.program_id(1)
    h0 = CONV_HALO - (width - 1)
    rows = CONV_HALO + tt
    for s in range(nb):
        @pl.when(ti == 0)
        def _():
            uext[s, h0:CONV_HALO, :] = st_ref[s]

        @pl.when(ti > 0)
        def _():
            uext[s, h0:CONV_HALO, :] = uext[s, h0 + tt:CONV_HALO + tt, :]

        uext[s, CONV_HALO:rows, :] = za_ref[s] * _sigmoid(zb_ref[s])
        for sh in range(1, SUBLANES):
            ushift[sh - 1] = uext[s, sh:sh + rows - SUBLANES, :]
        for c0 in range(0, c, cch):
            for r0 in range(0, tt, rch):
                acc = jnp.broadcast_to(b_ref[:, c0:c0 + cch], (rch, cch))
                for k in range(width):
                    sh = (h0 + k) % SUBLANES
                    lo = h0 + k - sh + r0
                    if sh == 0:
                        win = uext[s, lo:lo + rch, c0:c0 + cch]
                    else:
                        win = ushift[sh - 1, lo:lo + rch, c0:c0 + cch]
                    acc = acc + win * w_ref[k:k + 1, c0:c0 + cch]
                cbuf[s * tt + r0:s * tt + r0 + rch, c0:c0 + cch] = acc
        nst_ref[s] = uext[s, rows - (width - 1):rows, :]
    x = cbuf[...]
    mu = jnp.mean(x, axis=-1, keepdims=True)
    xc = x - mu
    var = jnp.mean(xc * xc, axis=-1, keepdims=True)
    y = xc * lax.rsqrt(var + EPS) * lg_ref[...] + lb_ref[...]
    ca_ref[...] = (y * _sigmoid(y)).astype(ca_ref.dtype)


def _conformer_conv(z, state, w_dw, b_dw, ln_g, ln_b, bsz, t, c):
    width = w_dw.shape[0]
    d_in = z.shape[1]
    z3 = z.reshape(bsz, t, d_in)
    if t >= 128:
        nb, tt = 1, 128
    else:
        nb, tt = _pick(bsz, max(1, 128 // t), 1), t
    assert t % tt == 0 and bsz % nb == 0 and (tt >= width - 1 or tt == t)
    assert (nb * tt) % BF16_SUBLANES == 0 and tt % SUBLANES == 0 and width - 1 <= CONV_HALO
    cch = _pick(c, 512, LANES)
    rch = _pick(tt, 64, SUBLANES)
    body = functools.partial(_conv_body, nb=nb, tt=tt, c=c, width=width, cch=cch, rch=rch)
    nt = t // tt
    ca, nst = pl.pallas_call(
        body,
        out_shape=(jax.ShapeDtypeStruct((bsz * t, c), BF16),
                   jax.ShapeDtypeStruct((bsz, width - 1, c), F32)),
        grid=(bsz // nb, nt),
        in_specs=[pl.BlockSpec((nb, tt, c), lambda b, i: (b, i, 0)),
                  pl.BlockSpec((nb, tt, c), lambda b, i: (b, i, 1)),
                  pl.BlockSpec((nb, width - 1, c), lambda b, i: (b, 0, 0)),
                  pl.BlockSpec((width, c), lambda b, i: (0, 0)),
                  pl.BlockSpec((1, c), lambda b, i: (0, 0)),
                  pl.BlockSpec((1, c), lambda b, i: (0, 0)),
                  pl.BlockSpec((1, c), lambda b, i: (0, 0))],
        out_specs=(pl.BlockSpec((nb * tt, c), lambda b, i: (b * nt + i, 0)),
                   pl.BlockSpec((nb, width - 1, c), lambda b, i: (b, 0, 0))),
        scratch_shapes=[pltpu.VMEM((nb, CONV_HALO + tt, c), F32),
                        pltpu.VMEM((SUBLANES - 1, CONV_HALO + tt - SUBLANES, c), F32),
                        pltpu.VMEM((nb * tt, c), F32)],
        compiler_params=_params("parallel", "arbitrary"),
        name="conformer_conv",
    )(z3, z3, state, w_dw, b_dw.reshape(1, c), ln_g.reshape(1, c), ln_b.reshape(1, c))
    return ca, nst


def _hgrn_consts(c, seg):
    nlev = int(math.log2(seg))
    assert 1 << nlev == seg and c % seg == 0
    t = np.arange(c)
    mats = np.zeros((nlev + 1, c, c), np.float32)
    mats[0] = (t[None, :] <= t[:, None]) & ((t[None, :] // seg) == (t[:, None] // seg))
    masks = np.zeros((nlev, c, c), np.float32)
    for lev in range(1, nlev + 1):
        half = 1 << (lev - 1)
        blk = t >> lev
        upper = ((t >> (lev - 1)) & 1).astype(bool)
        p = (blk << lev) + half - 1
        r = t[None, :]
        up_rows = (r > p[:, None]) & (r <= t[:, None])
        lo_rows = (r > t[:, None]) & (r <= p[:, None])
        mats[lev] = np.where(upper[:, None], up_rows, lo_rows)
        masks[lev - 1] = (blk[:, None] == blk[None, :]) & upper[:, None] & (~upper[None, :])
    return mats.reshape((nlev + 1) * c, c), masks


def _hgrn_body(zq_ref, zf_ref, zi_ref, zg_ref, lbl_ref, gn_ref, mall_ref, masks_ref, s0_ref,
               o_ref, sout_ref, s_scr, *, hb, n_chunks, c, seg, nlev, carry):
    dk = HGRN_HEAD_DIM
    nseg = c // seg
    seg_dtype = BF16 if seg % BF16_SUBLANES == 0 else F32

    l0 = lbl_ref[0:1, :]
    l1 = lbl_ref[1:2, :]
    lmax = jnp.maximum(l0, l1)
    e0 = jnp.exp(l0 - lmax)
    lb = e0 / (e0 + jnp.exp(l1 - lmax))
    gn = gn_ref[...]
    mall = mall_ref[...]
    ones = jnp.ones((seg, dk), seg_dtype)

    def chunk(r0, load_s, store_s):
        zq = zq_ref[pl.ds(r0, c), :]
        zf = zf_ref[pl.ds(r0, c), :]
        v = zi_ref[pl.ds(r0, c), :]
        zg = zg_ref[pl.ds(r0, c), :]
        f = lb + (1.0 - lb) * _sigmoid(zf)
        g = jnp.log(f)
        kk = 1.0 - f
        q = zq * _sigmoid(zq)
        gate = zg * _sigmoid(zg)
        g1 = g.astype(BF16)
        g2 = (g - g1.astype(F32)).astype(BF16)
        gp = jnp.concatenate([p[:, h * dk:(h + 1) * dk] for h in range(hb) for p in (g1, g2)], axis=1)
        e_all = _dot(mall, gp)
        for h in range(hb):
            sl = slice(h * dk, (h + 1) * dk)
            eh = e_all[:, 2 * h * dk:(2 * h + 1) * dk] + e_all[:, (2 * h + 1) * dk:(2 * h + 2) * dk]
            bcum = eh[0:c]
            qh, kh, vh = q[:, sl], kk[:, sl], v[:, sl]
            vm = vh.astype(BF16)
            a = None
            for lev in range(1, nlev + 1):
                e = jnp.exp(eh[lev * c:(lev + 1) * c])
                p = _dot_nt((qh * e).astype(BF16), (kh * e).astype(BF16))
                term = masks_ref[lev - 1] * p
                a = term if a is None else a + term
            d = jnp.sum(qh * kh, axis=-1, keepdims=True)
            o = _dot(a.astype(BF16), vm) + d * vh
            qb = qh * jnp.exp(bcum)
            gph = gp[:, 2 * h * dk:(2 * h + 2) * dk].astype(seg_dtype)
            inter = []
            for j in range(nseg):
                rs = slice(j * seg, (j + 1) * seg)
                s_prev = load_s(j, h)
                b_last = bcum[(j + 1) * seg - 1:(j + 1) * seg, :]
                kd = kh[rs] * jnp.exp(b_last - bcum[rs])
                inter.append(_dot(qb[rs].astype(seg_dtype), s_prev.astype(seg_dtype)))
                bl2 = _dot_tn(gph[rs], ones)
                bl = bl2[0:dk] + bl2[dk:2 * dk]
                store_s(j, h, jnp.exp(bl) * s_prev + _dot_tn(kd.astype(seg_dtype), vh[rs].astype(seg_dtype)))
            o = o + (inter[0] if nseg == 1 else jnp.concatenate(inter, axis=0))
            ms = jnp.mean(o * o, axis=-1, keepdims=True)
            o_ref[pl.ds(r0, c), sl] = o * lax.rsqrt(ms + EPS) * gn * gate[:, sl]

    if carry:
        ti = pl.program_id(2)

        @pl.when(ti == 0)
        def _():
            s_scr[...] = s0_ref[0]

        def load_s(j, h):
            return s_scr[h]

        def store_s(j, h, val):
            s_scr[h] = val

        def loop(n, cr):
            chunk(pl.multiple_of(n * c, c), load_s, store_s)
            return cr

        lax.fori_loop(0, n_chunks, loop, 0)

        @pl.when(ti == pl.num_programs(2) - 1)
        def _():
            sout_ref[0] = s_scr[...]
    else:
        for n in range(n_chunks):
            def load_s(j, h, n=n):
                return s0_ref[n * nseg + j, h]

            def store_s(j, h, val, n=n):
                sout_ref[n * nseg + j, h] = val

            chunk(n * c, load_s, store_s)


def _hgrn(z, lb_logits, g_norm, state, bsz, t, d_hgrn, col0):
    dk = HGRN_HEAD_DIM
    nh = d_hgrn // dk
    hb = min(HGRN_HEADS_PER_STEP, nh)
    c = HGRN_CHUNK
    wl = hb * dk
    assert nh % hb == 0 and col0 % wl == 0 and d_hgrn % wl == 0
    if t >= c:
        assert t % c == 0
        carry, seg = True, c
        rows = _pick(t, HGRN_ROWS, c)
        nseq = 1
    else:
        carry, seg = False, t
        rows = c
        nseq = rows // t
        assert bsz % nseq == 0
    n_chunks = rows // c
    nt = (nseq * t) // rows
    nlev = int(math.log2(seg))
    mats, masks = _hgrn_consts(c, seg)
    cb = col0 // wl
    kstride = d_hgrn // wl

    def zspec(k):
        return pl.BlockSpec((rows, wl), lambda b, h, i: (b * nt + i, cb + k * kstride + h))

    body = functools.partial(_hgrn_body, hb=hb, n_chunks=n_chunks, c=c, seg=seg, nlev=nlev, carry=carry)
    o, s_out = pl.pallas_call(
        body,
        out_shape=(jax.ShapeDtypeStruct((bsz * t, d_hgrn), F32),
                   jax.ShapeDtypeStruct((bsz, nh, dk, dk), F32)),
        grid=(bsz // nseq, nh // hb, nt),
        in_specs=[zspec(0), zspec(1), zspec(2), zspec(3),
                  pl.BlockSpec((2, wl), lambda b, h, i: (0, h)),
                  pl.BlockSpec((1, dk), lambda b, h, i: (0, 0)),
                  pl.BlockSpec(mats.shape, lambda b, h, i: (0, 0)),
                  pl.BlockSpec(masks.shape, lambda b, h, i: (0, 0, 0)),
                  pl.BlockSpec((nseq, hb, dk, dk), lambda b, h, i: (b, h, 0, 0))],
        out_specs=(pl.BlockSpec((rows, wl), lambda b, h, i: (b * nt + i, h)),
                   pl.BlockSpec((nseq, hb, dk, dk), lambda b, h, i: (b, h, 0, 0))),
        scratch_shapes=[pltpu.VMEM((hb, dk, dk), F32)],
        compiler_params=_params("parallel", "parallel", "arbitrary"),
        name="hgrn2",
    )(z, z, z, z, lb_logits, g_norm.reshape(1, dk), jnp.asarray(mats, BF16), jnp.asarray(masks), state)
    return o, s_out


def _merge_body(ca_ref, ob_ref, wc_ref, wh_ref, ga_ref, gb_ref, o_ref):
    ya = _dot(ca_ref[...], wc_ref[...])
    yb = _dot(ob_ref[...].astype(BF16), wh_ref[...])
    o_ref[...] = (_sigmoid(ga_ref[...]) * ya + _sigmoid(gb_ref[...]) * yb).astype(o_ref.dtype)


def _merge(ca, ob, w_conv_out, w_hgrn_out, z, gate_col0, d_model):
    m, kc = ca.shape
    kh = ob.shape[1]
    tm = _pick(m, 512, BF16_SUBLANES)
    tn = _pick(d_model, 512, LANES)
    assert gate_col0 % tn == 0
    ga0 = gate_col0 // tn
    gb0 = (gate_col0 + d_model) // tn
    return pl.pallas_call(
        _merge_body,
        out_shape=jax.ShapeDtypeStruct((m, d_model), BF16),
        grid=(m // tm, d_model // tn),
        in_specs=[pl.BlockSpec((tm, kc), lambda i, j: (i, 0)),
                  pl.BlockSpec((tm, kh), lambda i, j: (i, 0)),
                  pl.BlockSpec((kc, tn), lambda i, j: (0, j)),
                  pl.BlockSpec((kh, tn), lambda i, j: (0, j)),
                  pl.BlockSpec((tm, tn), lambda i, j: (i, ga0 + j)),
                  pl.BlockSpec((tm, tn), lambda i, j: (i, gb0 + j))],
        out_specs=pl.BlockSpec((tm, tn), lambda i, j: (i, j)),
        compiler_params=_params("parallel", "parallel"),
        name="mixer_merge",
    )(ca, ob, w_conv_out, w_hgrn_out, z, z)


def _xattn_body(q_ref, k_ref, v_ref, o_ref, *, nb, tq, nh, mxu_dtype):
    dh = XATTN_HEAD_DIM
    scale = dh ** -0.5
    rows = []
    for s in range(nb):
        q = q_ref[s]
        kmat = k_ref[s].astype(mxu_dtype)
        vmat = v_ref[s].astype(mxu_dtype)
        heads = []
        for h in range(nh):
            sl = slice(h * dh, (h + 1) * dh)
            sc = _dot_nt(q[:, sl].astype(mxu_dtype), kmat[:, sl]) * scale
            mx = jnp.max(sc, axis=-1, keepdims=True)
            p = jnp.exp(sc - mx)
            p = p / jnp.sum(p, axis=-1, keepdims=True)
            heads.append(_dot(p.astype(mxu_dtype), vmat[:, sl]))
        rows.append(jnp.concatenate(heads, axis=1))
    out = rows[0] if nb == 1 else jnp.concatenate(rows, axis=0)
    o_ref[...] = out.astype(o_ref.dtype)


def _xattn(qx, mem_k, mem_v, bsz, t):
    dx = qx.shape[1]
    n_mem = mem_k.shape[1]
    nh = dx // XATTN_HEAD_DIM
    if t >= 16:
        nb, tq = 1, _pick(t, 512, BF16_SUBLANES)
    else:
        nb, tq = _pick(bsz, max(1, 64 // t), 1), t
    assert (nb * tq) % BF16_SUBLANES == 0
    mxu_dtype = BF16 if tq % BF16_SUBLANES == 0 else F32
    nt = t // tq
    q3 = qx.reshape(bsz, t, dx)
    body = functools.partial(_xattn_body, nb=nb, tq=tq, nh=nh, mxu_dtype=mxu_dtype)
    return pl.pallas_call(
        body,
        out_shape=jax.ShapeDtypeStruct((bsz * t, dx), BF16),
        grid=(bsz // nb, nt),
        in_specs=[pl.BlockSpec((nb, tq, dx), lambda b, i: (b, i, 0)),
                  pl.BlockSpec((nb, n_mem, dx), lambda b, i: (b, 0, 0)),
                  pl.BlockSpec((nb, n_mem, dx), lambda b, i: (b, 0, 0))],
        out_specs=pl.BlockSpec((nb * tq, dx), lambda b, i: (b * nt + i, 0)),
        compiler_params=_params("parallel", "parallel"),
        name="xattn",
    )(q3, mem_k, mem_v)


def _ffn_up_body(h_ref, wa_ref, wv_ref, dw_ref, db_ref, st_ref, o_ref, nst_ref, carry,
                 *, nseq, rows_per_seq, pieces):
    i = pl.program_id(0)
    j = pl.program_id(1)
    h = h_ref[...]
    a = _dot(h, wa_ref[...])
    val = _dot(h, wv_ref[...])
    tm, tn = a.shape
    if pieces == 1:
        prev = st_ref[...]
    else:
        prev = jnp.where(i % pieces == 0, st_ref[...], carry[j][None])
    carry[j] = a[tm - 2:tm, :]
    p0 = jnp.broadcast_to(prev[:, 0:1, :], (nseq, rows_per_seq, tn)).reshape(tm, tn)
    p1 = jnp.broadcast_to(prev[:, 1:2, :], (nseq, rows_per_seq, tn)).reshape(tm, tn)
    tpos = lax.broadcasted_iota(jnp.int32, (tm, tn), 0) % rows_per_seq
    a1 = jnp.where(tpos == 0, p1, pltpu.roll(a, 1, 0))
    a2 = jnp.where(tpos == 0, p0, jnp.where(tpos == 1, p1, pltpu.roll(a, 2, 0)))
    ac = dw_ref[0:1, :] * a2 + dw_ref[1:2, :] * a1 + dw_ref[2:3, :] * a + db_ref[...]
    o_ref[...] = (ac * _sigmoid(ac) * val).astype(o_ref.dtype)
    nst_ref[...] = a.reshape(nseq, rows_per_seq, tn)[:, rows_per_seq - 2:rows_per_seq, :]


def _ffn_up(h, w_up, w_dw, b_dw, state, bsz, t, d_ff):
    m, d = h.shape
    tm = _pick(m, FFN_TM, BF16_SUBLANES)
    tn = _pick(d_ff, 256, LANES)
    if t >= tm:
        assert t % tm == 0
        nseq, rows_per_seq, pieces = 1, tm, t // tm
    else:
        assert tm % t == 0 and tm == m
        nseq, rows_per_seq, pieces = tm // t, t, 1
    nj = d_ff // tn
    n_tiles = m // tm
    body = functools.partial(_ffn_up_body, nseq=nseq, rows_per_seq=rows_per_seq, pieces=pieces)

    def seq_idx(i):
        return i // pieces if nseq == 1 else i

    hmid, tails = pl.pallas_call(
        body,
        out_shape=(jax.ShapeDtypeStruct((m, d_ff), BF16),
                   jax.ShapeDtypeStruct((n_tiles * nseq, 2, d_ff), F32)),
        grid=(n_tiles, nj),
        in_specs=[pl.BlockSpec((tm, d), lambda i, j: (i, 0)),
                  pl.BlockSpec((d, tn), lambda i, j: (0, j)),
                  pl.BlockSpec((d, tn), lambda i, j: (0, nj + j)),
                  pl.BlockSpec((3, tn), lambda i, j: (0, j)),
                  pl.BlockSpec((1, tn), lambda i, j: (0, j)),
                  pl.BlockSpec((nseq, 2, tn), lambda i, j: (seq_idx(i), 0, j))],
        out_specs=(pl.BlockSpec((tm, tn), lambda i, j: (i, j)),
                   pl.BlockSpec((nseq, 2, tn), lambda i, j: (i, 0, j))),
        scratch_shapes=[pltpu.VMEM((nj, 2, tn), F32)],
        compiler_params=_params("arbitrary", "arbitrary"),
        name="ffn_up",
    )(h, w_up, w_up, w_dw, b_dw.reshape(1, d_ff), state)
    if nseq == 1:
        new_state = tails.reshape(bsz, pieces, 2, d_ff)[:, pieces - 1]
    else:
        new_state = tails
    return hmid, new_state


def _layer(x, mem_k, mem_v, conv_st, hgrn_st, ffn_st, lb_logits, wts, bsz, t):
    d_model = x.shape[1]
    d_conv = wts["w_conv_out"].shape[0]
    d_hgrn = wts["w_hgrn_out"].shape[0]
    d_ff = wts["w_down"].shape[0]

    h = _rmsnorm(x, wts["g_mix"], BF16)
    z = _matmul(h, wts["w_in"], name="in_proj")
    ca, new_conv = _conformer_conv(z, conv_st, wts["w_conv_dw"], wts["b_conv_dw"],
                                   wts["ln_conv_g"], wts["ln_conv_b"], bsz, t, d_conv)
    ob, new_hgrn = _hgrn(z, lb_logits, wts["g_hgrn_norm"], hgrn_st, bsz, t, d_hgrn, 2 * d_conv)
    merged = _merge(ca, ob, wts["w_conv_out"], wts["w_hgrn_out"], z, 2 * d_conv + 4 * d_hgrn, d_model)
    x = _matmul(merged, wts["w_mix_out"], res=x, name="mix_out")

    h = _rmsnorm(x, wts["g_xattn"], BF16)
    qx = _matmul(h, wts["w_xq"], name="xattn_q")
    ox = _xattn(qx, mem_k, mem_v, bsz, t)
    x = _matmul(ox, wts["w_xo"], res=x, name="xattn_out")

    h = _rmsnorm(x, wts["g_ffn"], BF16)
    hmid, new_ffn = _ffn_up(h, wts["w_up"], wts["w_ffn_dw"], wts["b_ffn_dw"], ffn_st, bsz, t, d_ff)
    x = _matmul(hmid, wts["w_down"], res=x, tm=512, tn=256, name="ffn_down")
    return x, new_conv, new_hgrn, new_ffn


def kernel(x_prompt, x_sample, mem_prompt, state_conv, state_hgrn, cache_mem_k, cache_mem_v, state_ffn_conv, g_mix, w_in, w_conv_dw, b_conv_dw, ln_conv_g, ln_conv_b, w_conv_out, lb_logits, g_hgrn_norm, w_hgrn_out, w_mix_out, g_xattn, g_mem, w_xq, w_xk, w_xv, w_xo, g_ffn, w_up, w_ffn_dw, b_ffn_dw, w_down, g_final):
    depth = w_in.shape[0]
    assert depth == 1 and lb_logits.shape[0] == depth + 1
    bp, tp, d_model = x_prompt.shape
    bs, ts, _ = x_sample.shape
    n_mem = mem_prompt.shape[1]
    d_x = w_xq.shape[2]
    nhx = d_x // XATTN_HEAD_DIM
    d_conv = w_conv_out.shape[1]
    d_hgrn = w_hgrn_out.shape[1]
    d_ff = w_down.shape[1]
    dk = HGRN_HEAD_DIM
    nh = d_hgrn // dk
    cw = w_conv_dw.shape[1]

    def w2(w):
        return w.reshape(w.shape[1:]).astype(BF16)

    def v1(p):
        return p.reshape(p.shape[1:])

    wts = dict(
        g_mix=v1(g_mix), w_in=w2(w_in), w_conv_dw=v1(w_conv_dw), b_conv_dw=v1(b_conv_dw),
        ln_conv_g=v1(ln_conv_g), ln_conv_b=v1(ln_conv_b), w_conv_out=w2(w_conv_out),
        g_hgrn_norm=v1(g_hgrn_norm), w_hgrn_out=w2(w_hgrn_out), w_mix_out=w2(w_mix_out),
        g_xattn=v1(g_xattn), w_xq=w2(w_xq), w_xo=w2(w_xo), g_ffn=v1(g_ffn), w_up=w2(w_up),
        w_ffn_dw=v1(w_ffn_dw), b_ffn_dw=v1(b_ffn_dw), w_down=w2(w_down))

    mem_h = _rmsnorm(mem_prompt.reshape(bp * n_mem, d_model), v1(g_mem), BF16)
    mk = _matmul(mem_h, w2(w_xk), name="mem_k")
    mv = _matmul(mem_h, w2(w_xv), name="mem_v")
    xp, p_conv, p_hgrn, p_ffn = _layer(
        x_prompt.reshape(bp * tp, d_model), mk.reshape(bp, n_mem, d_x), mv.reshape(bp, n_mem, d_x),
        jnp.zeros((bp, cw - 1, d_conv), F32), jnp.zeros((bp, nh, dk, dk), F32),
        jnp.zeros((bp, 2, d_ff), F32), lb_logits, wts, bp, tp)
    xs, s_conv, s_hgrn, s_ffn = _layer(
        x_sample.reshape(bs * ts, d_model), cache_mem_k.reshape(bs, n_mem, d_x),
        cache_mem_v.reshape(bs, n_mem, d_x), state_conv.reshape(bs, cw - 1, d_conv),
        state_hgrn.reshape(bs, nh, dk, dk), state_ffn_conv.reshape(bs, 2, d_ff),
        lb_logits, wts, bs, ts)

    y_prompt = _rmsnorm(xp, g_final, F32).reshape(bp, tp, d_model)
    y_sample = _rmsnorm(xs, g_final, F32).reshape(bs, ts, d_model)
    return (y_prompt, y_sample,
            p_conv[None], p_hgrn[None],
            mk.reshape(1, bp, n_mem, nhx, XATTN_HEAD_DIM), mv.reshape(1, bp, n_mem, nhx, XATTN_HEAD_DIM),
            p_ffn[None],
            s_conv[None], s_hgrn[None], s_ffn[None])
```

```python
import functools
import math

import jax
import jax.numpy as jnp
import numpy as np
from jax import lax
from jax.experimental import pallas as pl
from jax.experimental.pallas import tpu as pltpu

F32 = jnp.float32
BF16 = jnp.bfloat16
EPS = 1e-6

LANES = 128
SUBLANES = 8
BF16_SUBLANES = 16
VMEM_LIMIT_BYTES = 56 * 1024 * 1024

HGRN_HEAD_DIM = 128
XATTN_HEAD_DIM = 128
HGRN_CHUNK = 128
HGRN_HEADS_PER_STEP = 4
HGRN_ROWS = 1024
FFN_TM = 1024
FFN_SUB = 512


def _pick(n, target, mult):
    if n <= target:
        return n
    best = None
    for d in range(mult, target + 1, mult):
        if n % d == 0:
            best = d
    assert best is not None, (n, target, mult)
    return best


def _params(*sem):
    return pltpu.CompilerParams(dimension_semantics=sem, vmem_limit_bytes=VMEM_LIMIT_BYTES)


def _sigmoid(x):
    return 1.0 / (1.0 + jnp.exp(-x))


def _dot(a, b):
    return jnp.dot(a, b, preferred_element_type=F32)


def _dot_nt(a, b):
    return lax.dot_general(a, b, (((1,), (1,)), ((), ())), preferred_element_type=F32)


def _dot_tn(a, b):
    return lax.dot_general(a, b, (((0,), (0,)), ((), ())), preferred_element_type=F32)


def _rmsnorm_body(x_ref, g_ref, o_ref):
    x = x_ref[...]
    ms = jnp.mean(x * x, axis=-1, keepdims=True)
    o_ref[...] = (x * lax.rsqrt(ms + EPS) * g_ref[...]).astype(o_ref.dtype)


def _rmsnorm(x, g, out_dtype):
    m, d = x.shape
    tr = _pick(m, 256, BF16_SUBLANES)
    return pl.pallas_call(
        _rmsnorm_body,
        out_shape=jax.ShapeDtypeStruct((m, d), out_dtype),
        grid=(m // tr,),
        in_specs=[pl.BlockSpec((tr, d), lambda i: (i, 0)),
                  pl.BlockSpec((1, d), lambda i: (0, 0))],
        out_specs=pl.BlockSpec((tr, d), lambda i: (i, 0)),
        compiler_params=_params("parallel"),
        name="rmsnorm",
    )(x, g.reshape(1, d))


def _mm_body(a_ref, w_ref, o_ref):
    o_ref[...] = _dot(a_ref[...], w_ref[...]).astype(o_ref.dtype)


def _mm_res_body(a_ref, w_ref, r_ref, o_ref):
    o_ref[...] = (r_ref[...] + _dot(a_ref[...], w_ref[...])).astype(o_ref.dtype)


def _matmul(a, w, res=None, out_dtype=F32, tm=1024, tn=1024, name="matmul"):
    m, k = a.shape
    n = w.shape[1]
    tm = _pick(m, tm, BF16_SUBLANES)
    tn = _pick(n, tn, LANES)
    in_specs = [pl.BlockSpec((tm, k), lambda i, j: (i, 0)),
                pl.BlockSpec((k, tn), lambda i, j: (0, j))]
    args = [a, w]
    body = _mm_body
    if res is not None:
        in_specs.append(pl.BlockSpec((tm, tn), lambda i, j: (i, j)))
        args.append(res)
        body = _mm_res_body
    return pl.pallas_call(
        body,
        out_shape=jax.ShapeDtypeStruct((m, n), out_dtype),
        grid=(m // tm, n // tn),
        in_specs=in_specs,
        out_specs=pl.BlockSpec((tm, tn), lambda i, j: (i, j)),
        compiler_params=_params("parallel", "parallel"),
        name=name,
    )(*args)


CONV_HALO = 32


def _conv_body(za_ref, zb_ref, st_ref, w_ref, b_ref, lg_ref, lb_ref, ca_ref, nst_ref, uext, ushift, cbuf,
               *, nb, tt, c, width, cch, rch):
    ti = pl.program_id(1)
    h0 = CONV_HALO - (width - 1)
    rows = CONV_HALO + tt
    for s in range(nb):
        @pl.when(ti == 0)
        def _():
            uext[s, h0:CONV_HALO, :] = st_ref[s]

        @pl.when(ti > 0)
        def _():
            uext[s, h0:CONV_HALO, :] = uext[s, h0 + tt:CONV_HALO + tt, :]

        uext[s, CONV_HALO:rows, :] = za_ref[s] * _sigmoid(zb_ref[s])
        for sh in range(1, SUBLANES):
            ushift[sh - 1] = uext[s, sh:sh + rows - SUBLANES, :]
        for c0 in range(0, c, cch):
            for r0 in range(0, tt, rch):
                acc = jnp.broadcast_to(b_ref[:, c0:c0 + cch], (rch, cch))
                for k in range(width):
                    sh = (h0 + k) % SUBLANES
                    lo = h0 + k - sh + r0
                    if sh == 0:
                        win = uext[s, lo:lo + rch, c0:c0 + cch]
                    else:
                        win = ushift[sh - 1, lo:lo + rch, c0:c0 + cch]
                    acc = acc + win * w_ref[k:k + 1, c0:c0 + cch]
                cbuf[s * tt + r0:s * tt + r0 + rch, c0:c0 + cch] = acc
        nst_ref[s] = uext[s, rows - (width - 1):rows, :]
    x = cbuf[...]
    mu = jnp.mean(x, axis=-1, keepdims=True)
    xc = x - mu
    var = jnp.mean(xc * xc, axis=-1, keepdims=True)
    y = xc * lax.rsqrt(var + EPS) * lg_ref[...] + lb_ref[...]
    ca_ref[...] = (y * _sigmoid(y)).astype(ca_ref.dtype)


def _conformer_conv(z, state, w_dw, b_dw, ln_g, ln_b, bsz, t, c):
    width = w_dw.shape[0]
    d_in = z.shape[1]
    z3 = z.reshape(bsz, t, d_in)
    if t >= 128:
        nb, tt = 1, 128
    else:
        nb, tt = _pick(bsz, max(1, 128 // t), 1), t
    assert t % tt == 0 and bsz % nb == 0 and (tt >= width - 1 or tt == t)
    assert (nb * tt) % BF16_SUBLANES == 0 and tt % SUBLANES == 0 and width - 1 <= CONV_HALO
    cch = _pick(c, 512, LANES)
    rch = _pick(tt, 64, SUBLANES)
    body = functools.partial(_conv_body, nb=nb, tt=tt, c=c, width=width, cch=cch, rch=rch)
    nt = t // tt
    ca, nst = pl.pallas_call(
        body,
        out_shape=(jax.ShapeDtypeStruct((bsz * t, c), BF16),
                   jax.ShapeDtypeStruct((bsz, width - 1, c), F32)),
        grid=(bsz // nb, nt),
        in_specs=[pl.BlockSpec((nb, tt, c), lambda b, i: (b, i, 0)),
                  pl.BlockSpec((nb, tt, c), lambda b, i: (b, i, 1)),
                  pl.BlockSpec((nb, width - 1, c), lambda b, i: (b, 0, 0)),
                  pl.BlockSpec((width, c), lambda b, i: (0, 0)),
                  pl.BlockSpec((1, c), lambda b, i: (0, 0)),
                  pl.BlockSpec((1, c), lambda b, i: (0, 0)),
                  pl.BlockSpec((1, c), lambda b, i: (0, 0))],
        out_specs=(pl.BlockSpec((nb * tt, c), lambda b, i: (b * nt + i, 0)),
                   pl.BlockSpec((nb, width - 1, c), lambda b, i: (b, 0, 0))),
        scratch_shapes=[pltpu.VMEM((nb, CONV_HALO + tt, c), F32),
                        pltpu.VMEM((SUBLANES - 1, CONV_HALO + tt - SUBLANES, c), F32),
                        pltpu.VMEM((nb * tt, c), F32)],
        compiler_params=_params("parallel", "arbitrary"),
        name="conformer_conv",
    )(z3, z3, state, w_dw, b_dw.reshape(1, c), ln_g.reshape(1, c), ln_b.reshape(1, c))
    return ca, nst


def _hgrn_consts(c, seg):
    nlev = int(math.log2(seg))
    assert 1 << nlev == seg and c % seg == 0
    t = np.arange(c)
    mats = np.zeros((nlev + 1, c, c), np.float32)
    mats[0] = (t[None, :] <= t[:, None]) & ((t[None, :] // seg) == (t[:, None] // seg))
    masks = np.zeros((nlev, c, c), np.float32)
    for lev in range(1, nlev + 1):
        half = 1 << (lev - 1)
        blk = t >> lev
        upper = ((t >> (lev - 1)) & 1).astype(bool)
        p = (blk << lev) + half - 1
        r = t[None, :]
        up_rows = (r > p[:, None]) & (r <= t[:, None])
        lo_rows = (r > t[:, None]) & (r <= p[:, None])
        mats[lev] = np.where(upper[:, None], up_rows, lo_rows)
        masks[lev - 1] = (blk[:, None] == blk[None, :]) & upper[:, None] & (~upper[None, :])
    return mats.reshape((nlev + 1) * c, c), masks


def _hgrn_body(zq_ref, zf_ref, zi_ref, zg_ref, lbl_ref, gn_ref, mall_ref, masks_ref, s0_ref,
               o_ref, sout_ref, s_scr, *, hb, n_chunks, c, seg, nlev, carry):
    dk = HGRN_HEAD_DIM
    nseg = c // seg
    seg_dtype = BF16 if seg % BF16_SUBLANES == 0 else F32

    l0 = lbl_ref[0:1, :]
    l1 = lbl_ref[1:2, :]
    lmax = jnp.maximum(l0, l1)
    e0 = jnp.exp(l0 - lmax)
    lb = e0 / (e0 + jnp.exp(l1 - lmax))
    gn = gn_ref[...]
    mall = mall_ref[...]
    ones = jnp.ones((seg, dk), seg_dtype)

    def chunk(r0, load_s, store_s):
        zq = zq_ref[pl.ds(r0, c), :]
        zf = zf_ref[pl.ds(r0, c), :]
        v = zi_ref[pl.ds(r0, c), :]
        zg = zg_ref[pl.ds(r0, c), :]
        f = lb + (1.0 - lb) * _sigmoid(zf)
        g = jnp.log(f)
        kk = 1.0 - f
        q = zq * _sigmoid(zq)
        gate = zg * _sigmoid(zg)
        g1 = g.astype(BF16)
        g2 = (g - g1.astype(F32)).astype(BF16)
        gp = jnp.concatenate([p[:, h * dk:(h + 1) * dk] for h in range(hb) for p in (g1, g2)], axis=1)
        e_all = _dot(mall, gp)
        for h in range(hb):
            sl = slice(h * dk, (h + 1) * dk)
            eh = e_all[:, 2 * h * dk:(2 * h + 1) * dk] + e_all[:, (2 * h + 1) * dk:(2 * h + 2) * dk]
            bcum = eh[0:c]
            qh, kh, vh = q[:, sl], kk[:, sl], v[:, sl]
            vm = vh.astype(BF16)
            a = None
            for lev in range(1, nlev + 1):
                e = jnp.exp(eh[lev * c:(lev + 1) * c])
                p = _dot_nt((qh * e).astype(BF16), (kh * e).astype(BF16))
                term = masks_ref[lev - 1] * p
                a = term if a is None else a + term
            d = jnp.sum(qh * kh, axis=-1, keepdims=True)
            o = _dot(a.astype(BF16), vm) + d * vh
            qb = qh * jnp.exp(bcum)
            gph = gp[:, 2 * h * dk:(2 * h + 2) * dk].astype(seg_dtype)
            inter = []
            for j in range(nseg):
                rs = slice(j * seg, (j + 1) * seg)
                s_prev = load_s(j, h)
                b_last = bcum[(j + 1) * seg - 1:(j + 1) * seg, :]
                kd = kh[rs] * jnp.exp(b_last - bcum[rs])
                inter.append(_dot(qb[rs].astype(seg_dtype), s_prev.astype(seg_dtype)))
                bl2 = _dot_tn(gph[rs], ones)
                bl = bl2[0:dk] + bl2[dk:2 * dk]
                store_s(j, h, jnp.exp(bl) * s_prev + _dot_tn(kd.astype(seg_dtype), vh[rs].astype(seg_dtype)))
            o = o + (inter[0] if nseg == 1 else jnp.concatenate(inter, axis=0))
            ms = jnp.mean(o * o, axis=-1, keepdims=True)
            o_ref[pl.ds(r0, c), sl] = (o * lax.rsqrt(ms + EPS) * gn * gate[:, sl]).astype(o_ref.dtype)

    if carry:
        ti = pl.program_id(2)

        @pl.when(ti == 0)
        def _():
            s_scr[...] = s0_ref[0]

        def load_s(j, h):
            return s_scr[h]

        def store_s(j, h, val):
            s_scr[h] = val

        def loop(n, cr):
            chunk(pl.multiple_of(n * c, c), load_s, store_s)
            return cr

        lax.fori_loop(0, n_chunks, loop, 0)

        @pl.when(ti == pl.num_programs(2) - 1)
        def _():
            sout_ref[0] = s_scr[...]
    else:
        for n in range(n_chunks):
            def load_s(j, h, n=n):
                return s0_ref[n * nseg + j, h]

            def store_s(j, h, val, n=n):
                sout_ref[n * nseg + j, h] = val

            chunk(n * c, load_s, store_s)


def _hgrn(z, lb_logits, g_norm, state, bsz, t, d_hgrn, col0):
    dk = HGRN_HEAD_DIM
    nh = d_hgrn // dk
    hb = min(HGRN_HEADS_PER_STEP, nh)
    c = HGRN_CHUNK
    wl = hb * dk
    assert nh % hb == 0 and col0 % wl == 0 and d_hgrn % wl == 0
    if t >= c:
        assert t % c == 0
        carry, seg = True, c
        rows = _pick(t, HGRN_ROWS, c)
        nseq = 1
    else:
        carry, seg = False, t
        rows = c
        nseq = rows // t
        assert bsz % nseq == 0
    n_chunks = rows // c
    nt = (nseq * t) // rows
    nlev = int(math.log2(seg))
    mats, masks = _hgrn_consts(c, seg)
    cb = col0 // wl
    kstride = d_hgrn // wl

    def zspec(k):
        return pl.BlockSpec((rows, wl), lambda b, h, i: (b * nt + i, cb + k * kstride + h))

    body = functools.partial(_hgrn_body, hb=hb, n_chunks=n_chunks, c=c, seg=seg, nlev=nlev, carry=carry)
    o, s_out = pl.pallas_call(
        body,
        out_shape=(jax.ShapeDtypeStruct((bsz * t, d_hgrn), BF16),
                   jax.ShapeDtypeStruct((bsz, nh, dk, dk), F32)),
        grid=(bsz // nseq, nh // hb, nt),
        in_specs=[zspec(0), zspec(1), zspec(2), zspec(3),
                  pl.BlockSpec((2, wl), lambda b, h, i: (0, h)),
                  pl.BlockSpec((1, dk), lambda b, h, i: (0, 0)),
                  pl.BlockSpec(mats.shape, lambda b, h, i: (0, 0)),
                  pl.BlockSpec(masks.shape, lambda b, h, i: (0, 0, 0)),
                  pl.BlockSpec((nseq, hb, dk, dk), lambda b, h, i: (b, h, 0, 0))],
        out_specs=(pl.BlockSpec((rows, wl), lambda b, h, i: (b * nt + i, h)),
                   pl.BlockSpec((nseq, hb, dk, dk), lambda b, h, i: (b, h, 0, 0))),
        scratch_shapes=[pltpu.VMEM((hb, dk, dk), F32)],
        compiler_params=_params("parallel", "parallel", "arbitrary"),
        name="hgrn2",
    )(z, z, z, z, lb_logits, g_norm.reshape(1, dk), jnp.asarray(mats, BF16), jnp.asarray(masks), state)
    return o, s_out


def _merge_body(ca_ref, ob_ref, wc_ref, wh_ref, ga_ref, gb_ref, o_ref):
    ya = _dot(ca_ref[...], wc_ref[...])
    yb = _dot(ob_ref[...], wh_ref[...])
    o_ref[...] = (_sigmoid(ga_ref[...]) * ya + _sigmoid(gb_ref[...]) * yb).astype(o_ref.dtype)


def _merge(ca, ob, w_conv_out, w_hgrn_out, z, gate_col0, d_model):
    m, kc = ca.shape
    kh = ob.shape[1]
    tm = _pick(m, 1024, BF16_SUBLANES)
    tn = _pick(d_model, 512, LANES)
    assert gate_col0 % tn == 0
    ga0 = gate_col0 // tn
    gb0 = (gate_col0 + d_model) // tn
    return pl.pallas_call(
        _merge_body,
        out_shape=jax.ShapeDtypeStruct((m, d_model), BF16),
        grid=(m // tm, d_model // tn),
        in_specs=[pl.BlockSpec((tm, kc), lambda i, j: (i, 0)),
                  pl.BlockSpec((tm, kh), lambda i, j: (i, 0)),
                  pl.BlockSpec((kc, tn), lambda i, j: (0, j)),
                  pl.BlockSpec((kh, tn), lambda i, j: (0, j)),
                  pl.BlockSpec((tm, tn), lambda i, j: (i, ga0 + j)),
                  pl.BlockSpec((tm, tn), lambda i, j: (i, gb0 + j))],
        out_specs=pl.BlockSpec((tm, tn), lambda i, j: (i, j)),
        compiler_params=_params("parallel", "parallel"),
        name="mixer_merge",
    )(ca, ob, w_conv_out, w_hgrn_out, z, z)


def _softmax_rows(sc):
    mx = jnp.max(sc, axis=-1, keepdims=True)
    p = jnp.exp(sc - mx)
    return p / jnp.sum(p, axis=-1, keepdims=True)


def _xblock_body(x_ref, g1_ref, wq_ref, k_ref, v_ref, wo_ref, g2_ref, x2_ref, h2_ref,
                 *, nb, tq, nh, head_rows):
    dh = XATTN_HEAD_DIM
    scale = dh ** -0.5
    x = x_ref[...]
    ms = jnp.mean(x * x, axis=-1, keepdims=True)
    h = (x * lax.rsqrt(ms + EPS) * g1_ref[...]).astype(BF16)
    qx = _dot(h, wq_ref[...])
    if head_rows:
        nk = k_ref.shape[1]
        row_head = lax.broadcasted_iota(jnp.int32, (nh * tq, nk), 0) // tq
        col_head = lax.broadcasted_iota(jnp.int32, (nh * tq, nk), 1) % nh
        same_head = row_head == col_head
    outs = []
    for s in range(nb):
        q = qx[s * tq:(s + 1) * tq]
        kmat = k_ref[s].astype(BF16)
        vmat = v_ref[s].astype(BF16)
        if head_rows:
            qs = jnp.concatenate([q[:, hd * dh:(hd + 1) * dh] for hd in range(nh)], axis=0)
            sc = jnp.where(same_head, _dot_nt(qs.astype(BF16), kmat) * scale, -1e30)
            o = _dot(_softmax_rows(sc).astype(BF16), vmat)
            outs.append(jnp.concatenate([o[hd * tq:(hd + 1) * tq] for hd in range(nh)], axis=1))
        else:
            heads = []
            for hd in range(nh):
                sl = slice(hd * dh, (hd + 1) * dh)
                sc = _dot_nt(q[:, sl].astype(BF16), kmat[:, sl]) * scale
                heads.append(_dot(_softmax_rows(sc).astype(BF16), vmat[:, sl]))
            outs.append(jnp.concatenate(heads, axis=1))
    ox = outs[0] if nb == 1 else jnp.concatenate(outs, axis=0)
    x2 = x + _dot(ox.astype(BF16), wo_ref[...])
    x2_ref[...] = x2
    ms2 = jnp.mean(x2 * x2, axis=-1, keepdims=True)
    h2_ref[...] = (x2 * lax.rsqrt(ms2 + EPS) * g2_ref[...]).astype(h2_ref.dtype)


def _xattn_block(x, g_in, w_xq, mem_k, mem_v, w_xo, g_next, bsz, t):
    m, d = x.shape
    dx = w_xq.shape[1]
    nh = dx // XATTN_HEAD_DIM
    nk, kw = mem_k.shape[1:]
    head_rows = kw == XATTN_HEAD_DIM and nh > 1
    if t >= 256:
        nb, tq = 1, 256
    else:
        nb, tq = _pick(bsz, max(1, 64 // t), 1), t
    assert t % tq == 0 and bsz % nb == 0
    assert (nb * tq) % BF16_SUBLANES == 0 and (nh * tq) % BF16_SUBLANES == 0
    nt = t // tq
    rows = nb * tq
    body = functools.partial(_xblock_body, nb=nb, tq=tq, nh=nh, head_rows=head_rows)
    return pl.pallas_call(
        body,
        out_shape=(jax.ShapeDtypeStruct((m, d), F32), jax.ShapeDtypeStruct((m, d), BF16)),
        grid=(bsz // nb, nt),
        in_specs=[pl.BlockSpec((rows, d), lambda b, i: (b * nt + i, 0)),
                  pl.BlockSpec((1, d), lambda b, i: (0, 0)),
                  pl.BlockSpec((d, dx), lambda b, i: (0, 0)),
                  pl.BlockSpec((nb, nk, kw), lambda b, i: (b, 0, 0)),
                  pl.BlockSpec((nb, nk, kw), lambda b, i: (b, 0, 0)),
                  pl.BlockSpec((dx, d), lambda b, i: (0, 0)),
                  pl.BlockSpec((1, d), lambda b, i: (0, 0))],
        out_specs=(pl.BlockSpec((rows, d), lambda b, i: (b * nt + i, 0)),
                   pl.BlockSpec((rows, d), lambda b, i: (b * nt + i, 0))),
        compiler_params=_params("parallel", "parallel"),
        name="xattn_block",
    )(x, g_in.reshape(1, d), w_xq, mem_k, mem_v, w_xo, g_next.reshape(1, d))


def _ffn_up_body(h_ref, wa_ref, wv_ref, dw_ref, db_ref, st_ref, o_ref, nst_ref, carry,
                 *, nseq, rows_per_seq, pieces, sub):
    i = pl.program_id(0)
    j = pl.program_id(1)
    tm = h_ref.shape[0]
    tn = wa_ref.shape[1]
    wa = wa_ref[...].astype(BF16)
    wv = wv_ref[...].astype(BF16)
    w0, w1, w2 = dw_ref[0:1, :], dw_ref[1:2, :], dw_ref[2:3, :]
    bias = db_ref[...]
    if nseq == 1:
        prev = st_ref[0] if pieces == 1 else jnp.where(i % pieces == 0, st_ref[0], carry[j])
        rps = sub
    else:
        rps = rows_per_seq
    nq = sub // rps
    tpos = lax.broadcasted_iota(jnp.int32, (sub, tn), 0) % rps
    for qi in range(tm // sub):
        hq = h_ref[qi * sub:(qi + 1) * sub, :]
        a = _dot(hq, wa)
        val = _dot(hq, wv)
        if nseq == 1:
            p0, p1 = prev[0:1, :], prev[1:2, :]
        else:
            st = st_ref[qi * nq:(qi + 1) * nq]
            p0 = jnp.broadcast_to(st[:, 0:1, :], (nq, rps, tn)).reshape(sub, tn)
            p1 = jnp.broadcast_to(st[:, 1:2, :], (nq, rps, tn)).reshape(sub, tn)
        a1 = jnp.where(tpos == 0, p1, pltpu.roll(a, 1, 0))
        a2 = jnp.where(tpos == 0, p0, jnp.where(tpos == 1, p1, pltpu.roll(a, 2, 0)))
        ac = w0 * a2 + w1 * a1 + w2 * a + bias
        o_ref[qi * sub:(qi + 1) * sub, :] = (ac * _sigmoid(ac) * val).astype(o_ref.dtype)
        if nseq == 1:
            prev = a[sub - 2:sub, :]
        else:
            nst_ref[qi * nq:(qi + 1) * nq] = a.reshape(nq, rps, tn)[:, rps - 2:rps, :]
    if nseq == 1:
        carry[j] = prev
        nst_ref[0] = prev


def _ffn_up(h, w_up, w_dw, b_dw, state, bsz, t, d_ff):
    m, d = h.shape
    tm = _pick(m, FFN_TM, BF16_SUBLANES)
    tn = _pick(d_ff, 256, LANES)
    if t >= tm:
        assert t % tm == 0
        nseq, rows_per_seq, pieces = 1, tm, t // tm
    else:
        assert tm % t == 0 and tm == m
        nseq, rows_per_seq, pieces = tm // t, t, 1
    nj = d_ff // tn
    n_tiles = m // tm
    sub = _pick(tm, FFN_SUB, BF16_SUBLANES)
    assert tm % sub == 0 and (nseq == 1 or sub % rows_per_seq == 0)
    body = functools.partial(_ffn_up_body, nseq=nseq, rows_per_seq=rows_per_seq, pieces=pieces, sub=sub)

    def seq_idx(i):
        return i // pieces if nseq == 1 else i

    hmid, tails = pl.pallas_call(
        body,
        out_shape=(jax.ShapeDtypeStruct((m, d_ff), BF16),
                   jax.ShapeDtypeStruct((n_tiles * nseq, 2, d_ff), F32)),
        grid=(n_tiles, nj),
        in_specs=[pl.BlockSpec((tm, d), lambda i, j: (i, 0)),
                  pl.BlockSpec((d, tn), lambda i, j: (0, j)),
                  pl.BlockSpec((d, tn), lambda i, j: (0, nj + j)),
                  pl.BlockSpec((3, tn), lambda i, j: (0, j)),
                  pl.BlockSpec((1, tn), lambda i, j: (0, j)),
                  pl.BlockSpec((nseq, 2, tn), lambda i, j: (seq_idx(i), 0, j))],
        out_specs=(pl.BlockSpec((tm, tn), lambda i, j: (i, j)),
                   pl.BlockSpec((nseq, 2, tn), lambda i, j: (i, 0, j))),
        scratch_shapes=[pltpu.VMEM((nj, 2, tn), F32)],
        compiler_params=_params("arbitrary", "arbitrary"),
        name="ffn_up",
    )(h, w_up, w_up, w_dw, b_dw.reshape(1, d_ff), state)
    if nseq == 1:
        new_state = tails.reshape(bsz, pieces, 2, d_ff)[:, pieces - 1]
    else:
        new_state = tails
    return hmid, new_state


def _layer(x, mem_k, mem_v, conv_st, hgrn_st, ffn_st, lb_logits, wts, bsz, t):
    d_model = x.shape[1]
    d_conv = wts["w_conv_out"].shape[0]
    d_hgrn = wts["w_hgrn_out"].shape[0]
    d_ff = wts["w_down"].shape[0]

    h = _rmsnorm(x, wts["g_mix"], BF16)
    z = _matmul(h, wts["w_in"], name="in_proj")
    ca, new_conv = _conformer_conv(z, conv_st, wts["w_conv_dw"], wts["b_conv_dw"],
                                   wts["ln_conv_g"], wts["ln_conv_b"], bsz, t, d_conv)
    ob, new_hgrn = _hgrn(z, lb_logits, wts["g_hgrn_norm"], hgrn_st, bsz, t, d_hgrn, 2 * d_conv)
    merged = _merge(ca, ob, wts["w_conv_out"], wts["w_hgrn_out"], z, 2 * d_conv + 4 * d_hgrn, d_model)
    x = _matmul(merged, wts["w_mix_out"], res=x, name="mix_out")

    x, h = _xattn_block(x, wts["g_xattn"], wts["w_xq"], mem_k, mem_v, wts["w_xo"], wts["g_ffn"], bsz, t)

    hmid, new_ffn = _ffn_up(h, wts["w_up"], wts["w_ffn_dw"], wts["b_ffn_dw"], ffn_st, bsz, t, d_ff)
    x = _matmul(hmid, wts["w_down"], res=x, tm=512, tn=256, name="ffn_down")
    return x, new_conv, new_hgrn, new_ffn


def kernel(x_prompt, x_sample, mem_prompt, state_conv, state_hgrn, cache_mem_k, cache_mem_v, state_ffn_conv, g_mix, w_in, w_conv_dw, b_conv_dw, ln_conv_g, ln_conv_b, w_conv_out, lb_logits, g_hgrn_norm, w_hgrn_out, w_mix_out, g_xattn, g_mem, w_xq, w_xk, w_xv, w_xo, g_ffn, w_up, w_ffn_dw, b_ffn_dw, w_down, g_final):
    depth = w_in.shape[0]
    assert depth == 1 and lb_logits.shape[0] == depth + 1
    bp, tp, d_model = x_prompt.shape
    bs, ts, _ = x_sample.shape
    n_mem = mem_prompt.shape[1]
    d_x = w_xq.shape[2]
    nhx = d_x // XATTN_HEAD_DIM
    d_conv = w_conv_out.shape[1]
    d_hgrn = w_hgrn_out.shape[1]
    d_ff = w_down.shape[1]
    dk = HGRN_HEAD_DIM
    nh = d_hgrn // dk
    cw = w_conv_dw.shape[1]

    def w2(w):
        return w.reshape(w.shape[1:]).astype(BF16)

    def v1(p):
        return p.reshape(p.shape[1:])

    wts = dict(
        g_mix=v1(g_mix), w_in=w2(w_in), w_conv_dw=v1(w_conv_dw), b_conv_dw=v1(b_conv_dw),
        ln_conv_g=v1(ln_conv_g), ln_conv_b=v1(ln_conv_b), w_conv_out=w2(w_conv_out),
        g_hgrn_norm=v1(g_hgrn_norm), w_hgrn_out=w2(w_hgrn_out), w_mix_out=w2(w_mix_out),
        g_xattn=v1(g_xattn), w_xq=w2(w_xq), w_xo=w2(w_xo), g_ffn=v1(g_ffn), w_up=v1(w_up),
        w_ffn_dw=v1(w_ffn_dw), b_ffn_dw=v1(b_ffn_dw), w_down=w2(w_down))

    mem_h = _rmsnorm(mem_prompt.reshape(bp * n_mem, d_model), v1(g_mem), BF16)
    mk = _matmul(mem_h, w2(w_xk), name="mem_k")
    mv = _matmul(mem_h, w2(w_xv), name="mem_v")
    xp, p_conv, p_hgrn, p_ffn = _layer(
        x_prompt.reshape(bp * tp, d_model), mk.reshape(bp, n_mem, d_x), mv.reshape(bp, n_mem, d_x),
        jnp.zeros((bp, cw - 1, d_conv), F32), jnp.zeros((bp, nh, dk, dk), F32),
        jnp.zeros((bp, 2, d_ff), F32), lb_logits, wts, bp, tp)
    xs, s_conv, s_hgrn, s_ffn = _layer(
        x_sample.reshape(bs * ts, d_model), cache_mem_k.reshape(bs, n_mem * nhx, XATTN_HEAD_DIM),
        cache_mem_v.reshape(bs, n_mem * nhx, XATTN_HEAD_DIM), state_conv.reshape(bs, cw - 1, d_conv),
        state_hgrn.reshape(bs, nh, dk, dk), state_ffn_conv.reshape(bs, 2, d_ff),
        lb_logits, wts, bs, ts)

    y_prompt = _rmsnorm(xp, g_final, F32).reshape(bp, tp, d_model)
    y_sample = _rmsnorm(xs, g_final, F32).reshape(bs, ts, d_model)
    return (y_prompt, y_sample,
            p_conv[None], p_hgrn[None],
            mk.reshape(1, bp, n_mem, nhx, XATTN_HEAD_DIM), mv.reshape(1, bp, n_mem, nhx, XATTN_HEAD_DIM),
            p_ffn[None],
            s_conv[None], s_hgrn[None], s_ffn[None])
```

```python
import functools
import math

import jax
import jax.numpy as jnp
import numpy as np
from jax import lax
from jax.experimental import pallas as pl
from jax.experimental.pallas import tpu as pltpu

F32 = jnp.float32
BF16 = jnp.bfloat16
EPS = 1e-6

LANES = 128
SUBLANES = 8
BF16_SUBLANES = 16
VMEM_LIMIT_BYTES = 56 * 1024 * 1024

HGRN_HEAD_DIM = 128
XATTN_HEAD_DIM = 128
HGRN_CHUNK = 128
HGRN_HEADS_PER_STEP = 4
HGRN_MXU_LEVELS = 3
HGRN_ROWS = 1024
FFN_TM = 2048
FFN_SUB = 512


def _pick(n, target, mult):
    if n <= target:
        return n
    best = None
    for d in range(mult, target + 1, mult):
        if n % d == 0:
            best = d
    assert best is not None, (n, target, mult)
    return best


def _params(*sem):
    return pltpu.CompilerParams(dimension_semantics=sem, vmem_limit_bytes=VMEM_LIMIT_BYTES)


def _sigmoid(x):
    return 1.0 / (1.0 + jnp.exp(-x))


def _dot(a, b):
    return jnp.dot(a, b, preferred_element_type=F32)


def _dot_nt(a, b):
    return lax.dot_general(a, b, (((1,), (1,)), ((), ())), preferred_element_type=F32)


def _dot_tn(a, b):
    return lax.dot_general(a, b, (((0,), (0,)), ((), ())), preferred_element_type=F32)


def _rmsnorm_body(x_ref, g_ref, o_ref):
    x = x_ref[...]
    ms = jnp.mean(x * x, axis=-1, keepdims=True)
    o_ref[...] = (x * lax.rsqrt(ms + EPS) * g_ref[...]).astype(o_ref.dtype)


def _rmsnorm(x, g, out_dtype):
    m, d = x.shape
    tr = _pick(m, 256, BF16_SUBLANES)
    return pl.pallas_call(
        _rmsnorm_body,
        out_shape=jax.ShapeDtypeStruct((m, d), out_dtype),
        grid=(m // tr,),
        in_specs=[pl.BlockSpec((tr, d), lambda i: (i, 0)),
                  pl.BlockSpec((1, d), lambda i: (0, 0))],
        out_specs=pl.BlockSpec((tr, d), lambda i: (i, 0)),
        compiler_params=_params("parallel"),
        name="rmsnorm",
    )(x, g.reshape(1, d))


def _mm_body(a_ref, w_ref, o_ref):
    o_ref[...] = _dot(a_ref[...], w_ref[...].astype(BF16)).astype(o_ref.dtype)


def _mm_res_body(a_ref, w_ref, r_ref, o_ref):
    o_ref[...] = (r_ref[...] + _dot(a_ref[...], w_ref[...].astype(BF16))).astype(o_ref.dtype)


def _matmul(a, w, res=None, out_dtype=F32, tm=1024, tn=1024, a_buffers=2, name="matmul"):
    m, k = a.shape
    n = w.shape[1]
    tm = _pick(m, tm, BF16_SUBLANES)
    tn = _pick(n, tn, LANES)
    a_mode = {} if a_buffers == 2 else dict(pipeline_mode=pl.Buffered(a_buffers))
    in_specs = [pl.BlockSpec((tm, k), lambda i, j: (i, 0), **a_mode),
                pl.BlockSpec((k, tn), lambda i, j: (0, j))]
    args = [a, w]
    body = _mm_body
    if res is not None:
        in_specs.append(pl.BlockSpec((tm, tn), lambda i, j: (i, j)))
        args.append(res)
        body = _mm_res_body
    return pl.pallas_call(
        body,
        out_shape=jax.ShapeDtypeStruct((m, n), out_dtype),
        grid=(m // tm, n // tn),
        in_specs=in_specs,
        out_specs=pl.BlockSpec((tm, tn), lambda i, j: (i, j)),
        compiler_params=_params("parallel", "parallel"),
        name=name,
    )(*args)


CONV_HALO = 32


def _conv_body(za_ref, zb_ref, st_ref, w_ref, b_ref, lg_ref, lb_ref, ca_ref, nst_ref, uext, ushift, cbuf,
               *, nb, tt, c, width, cch, rch):
    ti = pl.program_id(1)
    h0 = CONV_HALO - (width - 1)
    rows = CONV_HALO + tt
    for s in range(nb):
        @pl.when(ti == 0)
        def _():
            uext[s, h0:CONV_HALO, :] = st_ref[s]

        @pl.when(ti > 0)
        def _():
            uext[s, h0:CONV_HALO, :] = uext[s, h0 + tt:CONV_HALO + tt, :]

        uext[s, CONV_HALO:rows, :] = za_ref[s] * _sigmoid(zb_ref[s])
        for sh in range(1, SUBLANES):
            ushift[sh - 1] = uext[s, sh:sh + rows - SUBLANES, :]
        for c0 in range(0, c, cch):
            for r0 in range(0, tt, rch):
                acc = jnp.broadcast_to(b_ref[:, c0:c0 + cch], (rch, cch))
                for k in range(width):
                    sh = (h0 + k) % SUBLANES
                    lo = h0 + k - sh + r0
                    if sh == 0:
                        win = uext[s, lo:lo + rch, c0:c0 + cch]
                    else:
                        win = ushift[sh - 1, lo:lo + rch, c0:c0 + cch]
                    acc = acc + win * w_ref[k:k + 1, c0:c0 + cch]
                cbuf[s * tt + r0:s * tt + r0 + rch, c0:c0 + cch] = acc
        nst_ref[s] = uext[s, rows - (width - 1):rows, :]
    x = cbuf[...]
    mu = jnp.mean(x, axis=-1, keepdims=True)
    xc = x - mu
    var = jnp.mean(xc * xc, axis=-1, keepdims=True)
    y = xc * lax.rsqrt(var + EPS) * lg_ref[...] + lb_ref[...]
    ca_ref[...] = (y * _sigmoid(y)).astype(ca_ref.dtype)


def _conformer_conv(z, state, w_dw, b_dw, ln_g, ln_b, bsz, t, c):
    width = w_dw.shape[0]
    d_in = z.shape[1]
    z3 = z.reshape(bsz, t, d_in)
    if t >= 128:
        nb, tt = 1, 128
    else:
        nb, tt = _pick(bsz, max(1, 128 // t), 1), t
    assert t % tt == 0 and bsz % nb == 0 and (tt >= width - 1 or tt == t)
    assert (nb * tt) % BF16_SUBLANES == 0 and tt % SUBLANES == 0 and width - 1 <= CONV_HALO
    cch = _pick(c, 512, LANES)
    rch = _pick(tt, 64, SUBLANES)
    body = functools.partial(_conv_body, nb=nb, tt=tt, c=c, width=width, cch=cch, rch=rch)
    nt = t // tt
    ca, nst = pl.pallas_call(
        body,
        out_shape=(jax.ShapeDtypeStruct((bsz * t, c), BF16),
                   jax.ShapeDtypeStruct((bsz, width - 1, c), F32)),
        grid=(bsz // nb, nt),
        in_specs=[pl.BlockSpec((nb, tt, c), lambda b, i: (b, i, 0)),
                  pl.BlockSpec((nb, tt, c), lambda b, i: (b, i, 1)),
                  pl.BlockSpec((nb, width - 1, c), lambda b, i: (b, 0, 0)),
                  pl.BlockSpec((width, c), lambda b, i: (0, 0)),
                  pl.BlockSpec((1, c), lambda b, i: (0, 0)),
                  pl.BlockSpec((1, c), lambda b, i: (0, 0)),
                  pl.BlockSpec((1, c), lambda b, i: (0, 0))],
        out_specs=(pl.BlockSpec((nb * tt, c), lambda b, i: (b * nt + i, 0)),
                   pl.BlockSpec((nb, width - 1, c), lambda b, i: (b, 0, 0))),
        scratch_shapes=[pltpu.VMEM((nb, CONV_HALO + tt, c), F32),
                        pltpu.VMEM((SUBLANES - 1, CONV_HALO + tt - SUBLANES, c), F32),
                        pltpu.VMEM((nb * tt, c), F32)],
        compiler_params=_params("parallel", "arbitrary"),
        name="conformer_conv",
    )(z3, z3, state, w_dw, b_dw.reshape(1, c), ln_g.reshape(1, c), ln_b.reshape(1, c))
    return ca, nst


def _hgrn_consts(c, seg):
    nlev = int(math.log2(seg))
    assert 1 << nlev == seg and c % seg == 0
    t = np.arange(c)
    mxu_lev = min(nlev, HGRN_MXU_LEVELS)
    mats = np.zeros((mxu_lev + 1, c, c), np.float32)
    mats[0] = (t[None, :] <= t[:, None]) & ((t[None, :] // seg) == (t[:, None] // seg))
    masks = np.zeros((nlev, c, c), np.float32)
    for lev in range(1, nlev + 1):
        half = 1 << (lev - 1)
        blk = t >> lev
        upper = ((t >> (lev - 1)) & 1).astype(bool)
        p = (blk << lev) + half - 1
        r = t[None, :]
        if lev <= mxu_lev:
            up_rows = (r > p[:, None]) & (r <= t[:, None])
            lo_rows = (r > t[:, None]) & (r <= p[:, None])
            mats[lev] = np.where(upper[:, None], up_rows, lo_rows)
        masks[lev - 1] = (blk[:, None] == blk[None, :]) & upper[:, None] & (~upper[None, :])
    return mats.reshape((mxu_lev + 1) * c, c), masks


def _level_exponent(bcum, lev):
    half, size = 1 << (lev - 1), 1 << lev
    parts = []
    for b0 in range(0, bcum.shape[0], size):
        ref = bcum[b0 + half - 1:b0 + half, :]
        parts.append(ref - bcum[b0:b0 + half])
        parts.append(bcum[b0 + half:b0 + size] - ref)
    return jnp.concatenate(parts, axis=0)


def _hgrn_body(zq_ref, zf_ref, zi_ref, zg_ref, lbl_ref, gn_ref, mall_ref, masks_ref, s0_ref,
               o_ref, sout_ref, s_scr, *, hb, n_chunks, c, seg, nlev, carry):
    dk = HGRN_HEAD_DIM
    nseg = c // seg
    seg_dtype = BF16 if seg % BF16_SUBLANES == 0 else F32

    l0 = lbl_ref[0:1, :]
    l1 = lbl_ref[1:2, :]
    lmax = jnp.maximum(l0, l1)
    e0 = jnp.exp(l0 - lmax)
    lb = e0 / (e0 + jnp.exp(l1 - lmax))
    gn = gn_ref[...]
    mall = mall_ref[...]
    ones = jnp.ones((seg, dk), seg_dtype)

    def chunk(r0, load_s, store_s):
        zq = zq_ref[pl.ds(r0, c), :]
        zf = zf_ref[pl.ds(r0, c), :]
        v = zi_ref[pl.ds(r0, c), :]
        zg = zg_ref[pl.ds(r0, c), :]
        f = lb + (1.0 - lb) * _sigmoid(zf)
        g = jnp.log(f)
        kk = 1.0 - f
        q = zq * _sigmoid(zq)
        gate = zg * _sigmoid(zg)
        g1 = g.astype(BF16)
        g2 = (g - g1.astype(F32)).astype(BF16)
        gp = jnp.concatenate([p[:, h * dk:(h + 1) * dk] for h in range(hb) for p in (g1, g2)], axis=1)
        e_all = _dot(mall, gp)
        for h in range(hb):
            sl = slice(h * dk, (h + 1) * dk)
            eh = e_all[:, 2 * h * dk:(2 * h + 1) * dk] + e_all[:, (2 * h + 1) * dk:(2 * h + 2) * dk]
            bcum = eh[0:c]
            qh, kh, vh = q[:, sl], kk[:, sl], v[:, sl]
            vm = vh.astype(BF16)
            a = None
            for lev in range(1, nlev + 1):
                if lev <= HGRN_MXU_LEVELS:
                    e = jnp.exp(eh[lev * c:(lev + 1) * c])
                else:
                    e = jnp.exp(_level_exponent(bcum, lev))
                p = _dot_nt((qh * e).astype(BF16), (kh * e).astype(BF16))
                term = masks_ref[lev - 1] * p
                a = term if a is None else a + term
            d = jnp.sum(qh * kh, axis=-1, keepdims=True)
            o = _dot(a.astype(BF16), vm) + d * vh
            qb = qh * jnp.exp(bcum)
            gph = gp[:, 2 * h * dk:(2 * h + 2) * dk].astype(seg_dtype)
            inter = []
            for j in range(nseg):
                rs = slice(j * seg, (j + 1) * seg)
                s_prev = load_s(j, h)
                b_last = bcum[(j + 1) * seg - 1:(j + 1) * seg, :]
                kd = kh[rs] * jnp.exp(b_last - bcum[rs])
                inter.append(_dot(qb[rs].astype(seg_dtype), s_prev.astype(seg_dtype)))
                bl2 = _dot_tn(gph[rs], ones)
                bl = bl2[0:dk] + bl2[dk:2 * dk]
                store_s(j, h, jnp.exp(bl) * s_prev + _dot_tn(kd.astype(seg_dtype), vh[rs].astype(seg_dtype)))
            o = o + (inter[0] if nseg == 1 else jnp.concatenate(inter, axis=0))
            ms = jnp.mean(o * o, axis=-1, keepdims=True)
            o_ref[pl.ds(r0, c), sl] = (o * lax.rsqrt(ms + EPS) * gn * gate[:, sl]).astype(o_ref.dtype)

    if carry:
        ti = pl.program_id(2)

        @pl.when(ti == 0)
        def _():
            s_scr[...] = s0_ref[0]

        def load_s(j, h):
            return s_scr[h]

        def store_s(j, h, val):
            s_scr[h] = val

        def loop(n, cr):
            chunk(pl.multiple_of(n * c, c), load_s, store_s)
            return cr

        lax.fori_loop(0, n_chunks, loop, 0)

        @pl.when(ti == pl.num_programs(2) - 1)
        def _():
            sout_ref[0] = s_scr[...]
    else:
        for n in range(n_chunks):
            def load_s(j, h, n=n):
                return s0_ref[n * nseg + j, h]

            def store_s(j, h, val, n=n):
                sout_ref[n * nseg + j, h] = val

            chunk(n * c, load_s, store_s)


def _hgrn(z, lb_logits, g_norm, state, bsz, t, d_hgrn, col0):
    dk = HGRN_HEAD_DIM
    nh = d_hgrn // dk
    hb = min(HGRN_HEADS_PER_STEP, nh)
    c = HGRN_CHUNK
    wl = hb * dk
    assert nh % hb == 0 and col0 % wl == 0 and d_hgrn % wl == 0
    if t >= c:
        assert t % c == 0
        carry, seg = True, c
        rows = _pick(t, HGRN_ROWS, c)
        nseq = 1
    else:
        carry, seg = False, t
        rows = c
        nseq = rows // t
        assert bsz % nseq == 0
    n_chunks = rows // c
    nt = (nseq * t) // rows
    nlev = int(math.log2(seg))
    mats, masks = _hgrn_consts(c, seg)
    cb = col0 // wl
    kstride = d_hgrn // wl

    def zspec(k):
        return pl.BlockSpec((rows, wl), lambda b, h, i: (b * nt + i, cb + k * kstride + h))

    body = functools.partial(_hgrn_body, hb=hb, n_chunks=n_chunks, c=c, seg=seg, nlev=nlev, carry=carry)
    o, s_out = pl.pallas_call(
        body,
        out_shape=(jax.ShapeDtypeStruct((bsz * t, d_hgrn), BF16),
                   jax.ShapeDtypeStruct((bsz, nh, dk, dk), F32)),
        grid=(bsz // nseq, nh // hb, nt),
        in_specs=[zspec(0), zspec(1), zspec(2), zspec(3),
                  pl.BlockSpec((2, wl), lambda b, h, i: (0, h)),
                  pl.BlockSpec((1, dk), lambda b, h, i: (0, 0)),
                  pl.BlockSpec(mats.shape, lambda b, h, i: (0, 0)),
                  pl.BlockSpec(masks.shape, lambda b, h, i: (0, 0, 0)),
                  pl.BlockSpec((nseq, hb, dk, dk), lambda b, h, i: (b, h, 0, 0))],
        out_specs=(pl.BlockSpec((rows, wl), lambda b, h, i: (b * nt + i, h)),
                   pl.BlockSpec((nseq, hb, dk, dk), lambda b, h, i: (b, h, 0, 0))),
        scratch_shapes=[pltpu.VMEM((hb, dk, dk), F32)],
        compiler_params=_params("parallel", "parallel", "arbitrary"),
        name="hgrn2",
    )(z, z, z, z, lb_logits, g_norm.reshape(1, dk), jnp.asarray(mats, BF16), jnp.asarray(masks), state)
    return o, s_out


def _merge_body(ca_ref, ob_ref, wc_ref, wh_ref, ga_ref, gb_ref, o_ref):
    ya = _dot(ca_ref[...], wc_ref[...])
    yb = _dot(ob_ref[...], wh_ref[...])
    o_ref[...] = (_sigmoid(ga_ref[...]) * ya + _sigmoid(gb_ref[...]) * yb).astype(o_ref.dtype)


def _merge(ca, ob, w_conv_out, w_hgrn_out, z, gate_col0, d_model):
    m, kc = ca.shape
    kh = ob.shape[1]
    tm = _pick(m, 1024, BF16_SUBLANES)
    tn = _pick(d_model, 512, LANES)
    assert gate_col0 % tn == 0
    ga0 = gate_col0 // tn
    gb0 = (gate_col0 + d_model) // tn
    return pl.pallas_call(
        _merge_body,
        out_shape=jax.ShapeDtypeStruct((m, d_model), BF16),
        grid=(m // tm, d_model // tn),
        in_specs=[pl.BlockSpec((tm, kc), lambda i, j: (i, 0)),
                  pl.BlockSpec((tm, kh), lambda i, j: (i, 0)),
                  pl.BlockSpec((kc, tn), lambda i, j: (0, j)),
                  pl.BlockSpec((kh, tn), lambda i, j: (0, j)),
                  pl.BlockSpec((tm, tn), lambda i, j: (i, ga0 + j)),
                  pl.BlockSpec((tm, tn), lambda i, j: (i, gb0 + j))],
        out_specs=pl.BlockSpec((tm, tn), lambda i, j: (i, j)),
        compiler_params=_params("parallel", "parallel"),
        name="mixer_merge",
    )(ca, ob, w_conv_out, w_hgrn_out, z, z)


def _softmax_rows(sc):
    mx = jnp.max(sc, axis=-1, keepdims=True)
    p = jnp.exp(sc - mx)
    return p / jnp.sum(p, axis=-1, keepdims=True)


def _xblock_body(x_ref, g1_ref, wq_ref, k_ref, v_ref, wo_ref, g2_ref, x2_ref, h2_ref,
                 *, nb, tq, nh, head_rows):
    dh = XATTN_HEAD_DIM
    scale = dh ** -0.5
    x = x_ref[...]
    ms = jnp.mean(x * x, axis=-1, keepdims=True)
    h = (x * lax.rsqrt(ms + EPS) * g1_ref[...]).astype(BF16)
    qx = _dot(h, wq_ref[...])
    if head_rows:
        nk = k_ref.shape[1]
        row_head = lax.broadcasted_iota(jnp.int32, (nh * tq, nk), 0) // tq
        col_head = lax.broadcasted_iota(jnp.int32, (nh * tq, nk), 1) % nh
        same_head = row_head == col_head
    outs = []
    for s in range(nb):
        q = qx[s * tq:(s + 1) * tq]
        kmat = k_ref[s].astype(BF16)
        vmat = v_ref[s].astype(BF16)
        if head_rows:
            qs = jnp.concatenate([q[:, hd * dh:(hd + 1) * dh] for hd in range(nh)], axis=0)
            sc = jnp.where(same_head, _dot_nt(qs.astype(BF16), kmat) * scale, -1e30)
            o = _dot(_softmax_rows(sc).astype(BF16), vmat)
            outs.append(jnp.concatenate([o[hd * tq:(hd + 1) * tq] for hd in range(nh)], axis=1))
        else:
            heads = []
            for hd in range(nh):
                sl = slice(hd * dh, (hd + 1) * dh)
                sc = _dot_nt(q[:, sl].astype(BF16), kmat[:, sl]) * scale
                heads.append(_dot(_softmax_rows(sc).astype(BF16), vmat[:, sl]))
            outs.append(jnp.concatenate(heads, axis=1))
    ox = outs[0] if nb == 1 else jnp.concatenate(outs, axis=0)
    x2 = x + _dot(ox.astype(BF16), wo_ref[...])
    x2_ref[...] = x2
    ms2 = jnp.mean(x2 * x2, axis=-1, keepdims=True)
    h2_ref[...] = (x2 * lax.rsqrt(ms2 + EPS) * g2_ref[...]).astype(h2_ref.dtype)


def _xattn_block(x, g_in, w_xq, mem_k, mem_v, w_xo, g_next, bsz, t):
    m, d = x.shape
    dx = w_xq.shape[1]
    nh = dx // XATTN_HEAD_DIM
    nk, kw = mem_k.shape[1:]
    head_rows = kw == XATTN_HEAD_DIM and nh > 1
    if t >= 256:
        nb, tq = 1, 256
    else:
        nb, tq = _pick(bsz, max(1, 64 // t), 1), t
    assert t % tq == 0 and bsz % nb == 0
    assert (nb * tq) % BF16_SUBLANES == 0 and (nh * tq) % BF16_SUBLANES == 0
    nt = t // tq
    rows = nb * tq
    body = functools.partial(_xblock_body, nb=nb, tq=tq, nh=nh, head_rows=head_rows)
    return pl.pallas_call(
        body,
        out_shape=(jax.ShapeDtypeStruct((m, d), F32), jax.ShapeDtypeStruct((m, d), BF16)),
        grid=(bsz // nb, nt),
        in_specs=[pl.BlockSpec((rows, d), lambda b, i: (b * nt + i, 0)),
                  pl.BlockSpec((1, d), lambda b, i: (0, 0)),
                  pl.BlockSpec((d, dx), lambda b, i: (0, 0)),
                  pl.BlockSpec((nb, nk, kw), lambda b, i: (b, 0, 0)),
                  pl.BlockSpec((nb, nk, kw), lambda b, i: (b, 0, 0)),
                  pl.BlockSpec((dx, d), lambda b, i: (0, 0)),
                  pl.BlockSpec((1, d), lambda b, i: (0, 0))],
        out_specs=(pl.BlockSpec((rows, d), lambda b, i: (b * nt + i, 0)),
                   pl.BlockSpec((rows, d), lambda b, i: (b * nt + i, 0))),
        compiler_params=_params("parallel", "parallel"),
        name="xattn_block",
    )(x, g_in.reshape(1, d), w_xq, mem_k, mem_v, w_xo, g_next.reshape(1, d))


def _ffn_up_body(h_ref, wa_ref, wv_ref, dw_ref, db_ref, st_ref, o_ref, nst_ref, carry,
                 *, nseq, rows_per_seq, pieces, sub):
    i = pl.program_id(0)
    j = pl.program_id(1)
    tm = h_ref.shape[0]
    tn = wa_ref.shape[1]
    wa = wa_ref[...].astype(BF16)
    wv = wv_ref[...].astype(BF16)
    w0, w1, w2 = dw_ref[0:1, :], dw_ref[1:2, :], dw_ref[2:3, :]
    bias = db_ref[...]
    if nseq == 1:
        prev = st_ref[0] if pieces == 1 else jnp.where(i % pieces == 0, st_ref[0], carry[j])
        rps = sub
    else:
        rps = rows_per_seq
    nq = sub // rps
    tpos = lax.broadcasted_iota(jnp.int32, (sub, tn), 0) % rps
    for qi in range(tm // sub):
        hq = h_ref[qi * sub:(qi + 1) * sub, :]
        a = _dot(hq, wa)
        val = _dot(hq, wv)
        if nseq == 1:
            p0, p1 = prev[0:1, :], prev[1:2, :]
        else:
            st = st_ref[qi * nq:(qi + 1) * nq]
            p0 = jnp.broadcast_to(st[:, 0:1, :], (nq, rps, tn)).reshape(sub, tn)
            p1 = jnp.broadcast_to(st[:, 1:2, :], (nq, rps, tn)).reshape(sub, tn)
        a1 = jnp.where(tpos == 0, p1, pltpu.roll(a, 1, 0))
        a2 = jnp.where(tpos == 0, p0, jnp.where(tpos == 1, p1, pltpu.roll(a, 2, 0)))
        ac = w0 * a2 + w1 * a1 + w2 * a + bias
        o_ref[qi * sub:(qi + 1) * sub, :] = (ac * _sigmoid(ac) * val).astype(o_ref.dtype)
        if nseq == 1:
            prev = a[sub - 2:sub, :]
        else:
            nst_ref[qi * nq:(qi + 1) * nq] = a.reshape(nq, rps, tn)[:, rps - 2:rps, :]
    if nseq == 1:
        carry[j] = prev
        nst_ref[0] = prev


def _ffn_up(h, w_up, w_dw, b_dw, state, bsz, t, d_ff):
    m, d = h.shape
    tm = _pick(m, FFN_TM, BF16_SUBLANES)
    tn = _pick(d_ff, 256, LANES)
    if t >= tm:
        assert t % tm == 0
        nseq, rows_per_seq, pieces = 1, tm, t // tm
    else:
        assert tm % t == 0 and tm == m
        nseq, rows_per_seq, pieces = tm // t, t, 1
    nj = d_ff // tn
    n_tiles = m // tm
    sub = _pick(tm, FFN_SUB, BF16_SUBLANES)
    assert tm % sub == 0 and (nseq == 1 or sub % rows_per_seq == 0)
    body = functools.partial(_ffn_up_body, nseq=nseq, rows_per_seq=rows_per_seq, pieces=pieces, sub=sub)

    def seq_idx(i):
        return i // pieces if nseq == 1 else i

    hmid, tails = pl.pallas_call(
        body,
        out_shape=(jax.ShapeDtypeStruct((m, d_ff), BF16),
                   jax.ShapeDtypeStruct((n_tiles * nseq, 2, d_ff), F32)),
        grid=(n_tiles, nj),
        in_specs=[pl.BlockSpec((tm, d), lambda i, j: (i, 0), pipeline_mode=pl.Buffered(1)),
                  pl.BlockSpec((d, tn), lambda i, j: (0, j)),
                  pl.BlockSpec((d, tn), lambda i, j: (0, nj + j)),
                  pl.BlockSpec((3, tn), lambda i, j: (0, j)),
                  pl.BlockSpec((1, tn), lambda i, j: (0, j)),
                  pl.BlockSpec((nseq, 2, tn), lambda i, j: (seq_idx(i), 0, j))],
        out_specs=(pl.BlockSpec((tm, tn), lambda i, j: (i, j)),
                   pl.BlockSpec((nseq, 2, tn), lambda i, j: (i, 0, j))),
        scratch_shapes=[pltpu.VMEM((nj, 2, tn), F32)],
        compiler_params=_params("arbitrary", "arbitrary"),
        name="ffn_up",
    )(h, w_up, w_up, w_dw, b_dw.reshape(1, d_ff), state)
    if nseq == 1:
        new_state = tails.reshape(bsz, pieces, 2, d_ff)[:, pieces - 1]
    else:
        new_state = tails
    return hmid, new_state


def _layer(x, mem_k, mem_v, conv_st, hgrn_st, ffn_st, lb_logits, wts, bsz, t):
    d_model = x.shape[1]
    d_conv = wts["w_conv_out"].shape[0]
    d_hgrn = wts["w_hgrn_out"].shape[0]
    d_ff = wts["w_down"].shape[0]

    h = _rmsnorm(x, wts["g_mix"], BF16)
    z = _matmul(h, wts["w_in"], tm=2048, tn=512, a_buffers=1, name="in_proj")
    ca, new_conv = _conformer_conv(z, conv_st, wts["w_conv_dw"], wts["b_conv_dw"],
                                   wts["ln_conv_g"], wts["ln_conv_b"], bsz, t, d_conv)
    ob, new_hgrn = _hgrn(z, lb_logits, wts["g_hgrn_norm"], hgrn_st, bsz, t, d_hgrn, 2 * d_conv)
    merged = _merge(ca, ob, wts["w_conv_out"], wts["w_hgrn_out"], z, 2 * d_conv + 4 * d_hgrn, d_model)
    x = _matmul(merged, wts["w_mix_out"], res=x, name="mix_out")

    x, h = _xattn_block(x, wts["g_xattn"], wts["w_xq"], mem_k, mem_v, wts["w_xo"], wts["g_ffn"], bsz, t)

    hmid, new_ffn = _ffn_up(h, wts["w_up"], wts["w_ffn_dw"], wts["b_ffn_dw"], ffn_st, bsz, t, d_ff)
    x = _matmul(hmid, wts["w_down"], res=x, tm=512, tn=256, name="ffn_down")
    return x, new_conv, new_hgrn, new_ffn


def kernel(x_prompt, x_sample, mem_prompt, state_conv, state_hgrn, cache_mem_k, cache_mem_v, state_ffn_conv, g_mix, w_in, w_conv_dw, b_conv_dw, ln_conv_g, ln_conv_b, w_conv_out, lb_logits, g_hgrn_norm, w_hgrn_out, w_mix_out, g_xattn, g_mem, w_xq, w_xk, w_xv, w_xo, g_ffn, w_up, w_ffn_dw, b_ffn_dw, w_down, g_final):
    depth = w_in.shape[0]
    assert depth == 1 and lb_logits.shape[0] == depth + 1
    bp, tp, d_model = x_prompt.shape
    bs, ts, _ = x_sample.shape
    n_mem = mem_prompt.shape[1]
    d_x = w_xq.shape[2]
    nhx = d_x // XATTN_HEAD_DIM
    d_conv = w_conv_out.shape[1]
    d_hgrn = w_hgrn_out.shape[1]
    d_ff = w_down.shape[1]
    dk = HGRN_HEAD_DIM
    nh = d_hgrn // dk
    cw = w_conv_dw.shape[1]

    def w2(w):
        return w.reshape(w.shape[1:]).astype(BF16)

    def v1(p):
        return p.reshape(p.shape[1:])

    wts = dict(
        g_mix=v1(g_mix), w_in=v1(w_in), w_conv_dw=v1(w_conv_dw), b_conv_dw=v1(b_conv_dw),
        ln_conv_g=v1(ln_conv_g), ln_conv_b=v1(ln_conv_b), w_conv_out=w2(w_conv_out),
        g_hgrn_norm=v1(g_hgrn_norm), w_hgrn_out=w2(w_hgrn_out), w_mix_out=w2(w_mix_out),
        g_xattn=v1(g_xattn), w_xq=w2(w_xq), w_xo=w2(w_xo), g_ffn=v1(g_ffn), w_up=v1(w_up),
        w_ffn_dw=v1(w_ffn_dw), b_ffn_dw=v1(b_ffn_dw), w_down=w2(w_down))

    mem_h = _rmsnorm(mem_prompt.reshape(bp * n_mem, d_model), v1(g_mem), BF16)
    mk = _matmul(mem_h, w2(w_xk), name="mem_k")
    mv = _matmul(mem_h, w2(w_xv), name="mem_v")
    xp, p_conv, p_hgrn, p_ffn = _layer(
        x_prompt.reshape(bp * tp, d_model), mk.reshape(bp, n_mem, d_x), mv.reshape(bp, n_mem, d_x),
        jnp.zeros((bp, cw - 1, d_conv), F32), jnp.zeros((bp, nh, dk, dk), F32),
        jnp.zeros((bp, 2, d_ff), F32), lb_logits, wts, bp, tp)
    xs, s_conv, s_hgrn, s_ffn = _layer(
        x_sample.reshape(bs * ts, d_model), cache_mem_k.reshape(bs, n_mem * nhx, XATTN_HEAD_DIM),
        cache_mem_v.reshape(bs, n_mem * nhx, XATTN_HEAD_DIM), state_conv.reshape(bs, cw - 1, d_conv),
        state_hgrn.reshape(bs, nh, dk, dk), state_ffn_conv.reshape(bs, 2, d_ff),
        lb_logits, wts, bs, ts)

    y_prompt = _rmsnorm(xp, g_final, F32).reshape(bp, tp, d_model)
    y_sample = _rmsnorm(xs, g_final, F32).reshape(bs, ts, d_model)
    return (y_prompt, y_sample,
            p_conv[None], p_hgrn[None],
            mk.reshape(1, bp, n_mem, nhx, XATTN_HEAD_DIM), mv.reshape(1, bp, n_mem, nhx, XATTN_HEAD_DIM),
            p_ffn[None],
            s_conv[None], s_hgrn[None], s_ffn[None])
```

```python
import functools
import math

import jax
import jax.numpy as jnp
import numpy as np
from jax import lax
from jax.experimental import pallas as pl
from jax.experimental.pallas import tpu as pltpu

F32 = jnp.float32
BF16 = jnp.bfloat16
EPS = 1e-6
LOG2E = math.log2(math.e)

LANES = 128
SUBLANES = 8
BF16_SUBLANES = 16
VMEM_LIMIT_BYTES = 56 * 1024 * 1024

HGRN_HEAD_DIM = 128
XATTN_HEAD_DIM = 128
HGRN_CHUNK = 128
HGRN_HEADS_PER_STEP = 8
HGRN_MXU_LEVELS = 3
HGRN_ROWS = 1024
FFN_TM = 2048
FFN_SUB = 512


def _pick(n, target, mult):
    if n <= target:
        return n
    best = None
    for d in range(mult, target + 1, mult):
        if n % d == 0:
            best = d
    assert best is not None, (n, target, mult)
    return best


def _params(*sem):
    return pltpu.CompilerParams(dimension_semantics=sem, vmem_limit_bytes=VMEM_LIMIT_BYTES)


def _sigmoid(x):
    return 1.0 / (1.0 + jnp.exp(-x))


def _dot(a, b):
    return jnp.dot(a, b, preferred_element_type=F32)


def _dot_nt(a, b):
    return lax.dot_general(a, b, (((1,), (1,)), ((), ())), preferred_element_type=F32)


def _dot_tn(a, b):
    return lax.dot_general(a, b, (((0,), (0,)), ((), ())), preferred_element_type=F32)


def _rmsnorm_body(x_ref, g_ref, o_ref):
    x = x_ref[...]
    ms = jnp.mean(x * x, axis=-1, keepdims=True)
    o_ref[...] = (x * lax.rsqrt(ms + EPS) * g_ref[...]).astype(o_ref.dtype)


def _rmsnorm(x, g, out_dtype):
    m, d = x.shape
    tr = _pick(m, 512, BF16_SUBLANES)
    return pl.pallas_call(
        _rmsnorm_body,
        out_shape=jax.ShapeDtypeStruct((m, d), out_dtype),
        grid=(m // tr,),
        in_specs=[pl.BlockSpec((tr, d), lambda i: (i, 0)),
                  pl.BlockSpec((1, d), lambda i: (0, 0))],
        out_specs=pl.BlockSpec((tr, d), lambda i: (i, 0)),
        compiler_params=_params("parallel"),
        name="rmsnorm",
    )(x, g.reshape(1, d))


def _mm_body(a_ref, w_ref, o_ref):
    o_ref[...] = _dot(a_ref[...], w_ref[...].astype(BF16)).astype(o_ref.dtype)


def _mm_res_body(a_ref, w_ref, r_ref, o_ref):
    o_ref[...] = (r_ref[...] + _dot(a_ref[...], w_ref[...].astype(BF16))).astype(o_ref.dtype)


def _matmul(a, w, res=None, out_dtype=F32, tm=1024, tn=1024, a_buffers=2, name="matmul"):
    m, k = a.shape
    n = w.shape[1]
    tm = _pick(m, tm, BF16_SUBLANES)
    tn = _pick(n, tn, LANES)
    a_mode = {} if a_buffers == 2 else dict(pipeline_mode=pl.Buffered(a_buffers))
    in_specs = [pl.BlockSpec((tm, k), lambda i, j: (i, 0), **a_mode),
                pl.BlockSpec((k, tn), lambda i, j: (0, j))]
    args = [a, w]
    body = _mm_body
    if res is not None:
        in_specs.append(pl.BlockSpec((tm, tn), lambda i, j: (i, j)))
        args.append(res)
        body = _mm_res_body
    return pl.pallas_call(
        body,
        out_shape=jax.ShapeDtypeStruct((m, n), out_dtype),
        grid=(m // tm, n // tn),
        in_specs=in_specs,
        out_specs=pl.BlockSpec((tm, tn), lambda i, j: (i, j)),
        compiler_params=_params("parallel", "parallel"),
        name=name,
    )(*args)


CONV_HALO = 32


def _conv_body(za_ref, zb_ref, st_ref, w_ref, b_ref, lg_ref, lb_ref, ca_ref, nst_ref, uext, ushift, cbuf,
               *, nb, tt, c, width, cch, rch):
    ti = pl.program_id(1)
    h0 = CONV_HALO - (width - 1)
    rows = CONV_HALO + tt
    for s in range(nb):
        @pl.when(ti == 0)
        def _():
            uext[s, h0:CONV_HALO, :] = st_ref[s]

        @pl.when(ti > 0)
        def _():
            uext[s, h0:CONV_HALO, :] = uext[s, h0 + tt:CONV_HALO + tt, :]

        uext[s, CONV_HALO:rows, :] = za_ref[s] * _sigmoid(zb_ref[s])
        for sh in range(1, SUBLANES):
            ushift[sh - 1] = uext[s, sh:sh + rows - SUBLANES, :]
        for c0 in range(0, c, cch):
            for r0 in range(0, tt, rch):
                acc = jnp.broadcast_to(b_ref[:, c0:c0 + cch], (rch, cch))
                for k in range(width):
                    sh = (h0 + k) % SUBLANES
                    lo = h0 + k - sh + r0
                    if sh == 0:
                        win = uext[s, lo:lo + rch, c0:c0 + cch]
                    else:
                        win = ushift[sh - 1, lo:lo + rch, c0:c0 + cch]
                    acc = acc + win * w_ref[k:k + 1, c0:c0 + cch]
                cbuf[s * tt + r0:s * tt + r0 + rch, c0:c0 + cch] = acc
        nst_ref[s] = uext[s, rows - (width - 1):rows, :]
    x = cbuf[...]
    mu = jnp.mean(x, axis=-1, keepdims=True)
    xc = x - mu
    var = jnp.mean(xc * xc, axis=-1, keepdims=True)
    y = xc * lax.rsqrt(var + EPS) * lg_ref[...] + lb_ref[...]
    ca_ref[...] = (y * _sigmoid(y)).astype(ca_ref.dtype)


def _conformer_conv(z, state, w_dw, b_dw, ln_g, ln_b, bsz, t, c):
    width = w_dw.shape[0]
    d_in = z.shape[1]
    z3 = z.reshape(bsz, t, d_in)
    if t >= 128:
        nb, tt = 1, 128
    else:
        nb, tt = _pick(bsz, max(1, 128 // t), 1), t
    assert t % tt == 0 and bsz % nb == 0 and (tt >= width - 1 or tt == t)
    assert (nb * tt) % BF16_SUBLANES == 0 and tt % SUBLANES == 0 and width - 1 <= CONV_HALO
    cch = _pick(c, 512, LANES)
    rch = _pick(tt, 64, SUBLANES)
    body = functools.partial(_conv_body, nb=nb, tt=tt, c=c, width=width, cch=cch, rch=rch)
    nt = t // tt
    ca, nst = pl.pallas_call(
        body,
        out_shape=(jax.ShapeDtypeStruct((bsz * t, c), BF16),
                   jax.ShapeDtypeStruct((bsz, width - 1, c), F32)),
        grid=(bsz // nb, nt),
        in_specs=[pl.BlockSpec((nb, tt, c), lambda b, i: (b, i, 0)),
                  pl.BlockSpec((nb, tt, c), lambda b, i: (b, i, 1)),
                  pl.BlockSpec((nb, width - 1, c), lambda b, i: (b, 0, 0)),
                  pl.BlockSpec((width, c), lambda b, i: (0, 0)),
                  pl.BlockSpec((1, c), lambda b, i: (0, 0)),
                  pl.BlockSpec((1, c), lambda b, i: (0, 0)),
                  pl.BlockSpec((1, c), lambda b, i: (0, 0))],
        out_specs=(pl.BlockSpec((nb * tt, c), lambda b, i: (b * nt + i, 0)),
                   pl.BlockSpec((nb, width - 1, c), lambda b, i: (b, 0, 0))),
        scratch_shapes=[pltpu.VMEM((nb, CONV_HALO + tt, c), F32),
                        pltpu.VMEM((SUBLANES - 1, CONV_HALO + tt - SUBLANES, c), F32),
                        pltpu.VMEM((nb * tt, c), F32)],
        compiler_params=_params("parallel", "arbitrary"),
        name="conformer_conv",
    )(z3, z3, state, w_dw, b_dw.reshape(1, c), ln_g.reshape(1, c), ln_b.reshape(1, c))
    return ca, nst


def _hgrn_consts(c, seg):
    nlev = int(math.log2(seg))
    assert 1 << nlev == seg and c % seg == 0
    t = np.arange(c)
    mxu_lev = min(nlev, HGRN_MXU_LEVELS)
    mats = np.zeros((mxu_lev + 1, c, c), np.float32)
    mats[0] = (t[None, :] <= t[:, None]) & ((t[None, :] // seg) == (t[:, None] // seg))
    masks = np.zeros((nlev, c, c), np.float32)
    for lev in range(1, nlev + 1):
        half = 1 << (lev - 1)
        blk = t >> lev
        upper = ((t >> (lev - 1)) & 1).astype(bool)
        p = (blk << lev) + half - 1
        r = t[None, :]
        if lev <= mxu_lev:
            up_rows = (r > p[:, None]) & (r <= t[:, None])
            lo_rows = (r > t[:, None]) & (r <= p[:, None])
            mats[lev] = np.where(upper[:, None], up_rows, lo_rows)
        masks[lev - 1] = (blk[:, None] == blk[None, :]) & upper[:, None] & (~upper[None, :])
    return mats.reshape((mxu_lev + 1) * c, c), masks


def _level_exponent(bcum, lev):
    half, size = 1 << (lev - 1), 1 << lev
    parts = []
    for b0 in range(0, bcum.shape[0], size):
        ref = bcum[b0 + half - 1:b0 + half, :]
        parts.append(ref - bcum[b0:b0 + half])
        parts.append(bcum[b0 + half:b0 + size] - ref)
    return jnp.concatenate(parts, axis=0)


def _hgrn_body(zq_ref, zf_ref, zi_ref, zg_ref, lbl_ref, gn_ref, mall_ref, masks_ref, s0_ref,
               o_ref, sout_ref, s_scr, *, hb, n_chunks, c, seg, nlev, carry):
    dk = HGRN_HEAD_DIM
    nseg = c // seg
    seg_dtype = BF16 if seg % BF16_SUBLANES == 0 else F32

    l0 = lbl_ref[0:1, :]
    l1 = lbl_ref[1:2, :]
    lmax = jnp.maximum(l0, l1)
    e0 = jnp.exp(l0 - lmax)
    lb = e0 / (e0 + jnp.exp(l1 - lmax))
    gn = gn_ref[...]
    mall = mall_ref[...]
    ones = jnp.ones((2 * seg, dk), seg_dtype)

    def chunk(r0, load_s, store_s):
        zq = zq_ref[pl.ds(r0, c), :]
        zf = zf_ref[pl.ds(r0, c), :]
        v = zi_ref[pl.ds(r0, c), :]
        zg = zg_ref[pl.ds(r0, c), :]
        f = lb + (1.0 - lb) * _sigmoid(zf)
        g = jnp.log(f) * LOG2E
        kk = 1.0 - f
        q = zq * _sigmoid(zq)
        gate = zg * _sigmoid(zg)
        g1 = g.astype(BF16)
        g2 = (g - g1.astype(F32)).astype(BF16)
        e_all = _dot(mall, jnp.concatenate([g1, g2], axis=0))
        g1s, g2s = g1.astype(seg_dtype), g2.astype(seg_dtype)
        sls = [slice(h * dk, (h + 1) * dk) for h in range(hb)]
        bcum = e_all[0:c]
        vm = v.astype(BF16)
        a = [None] * hb
        for lev in range(1, nlev + 1):
            if lev <= HGRN_MXU_LEVELS:
                e = jnp.exp2(e_all[lev * c:(lev + 1) * c])
            else:
                e = jnp.exp2(_level_exponent(bcum, lev))
            qe = (q * e).astype(BF16)
            ke = (kk * e).astype(BF16)
            for h, sl in enumerate(sls):
                term = masks_ref[lev - 1] * _dot_nt(qe[:, sl], ke[:, sl])
                a[h] = term if a[h] is None else a[h] + term
        qk = q * kk
        qb = q * jnp.exp2(bcum)
        o = [_dot(a[h].astype(BF16), vm[:, sl]) + jnp.sum(qk[:, sl], axis=-1, keepdims=True) * v[:, sl]
             for h, sl in enumerate(sls)]
        for h, sl in enumerate(sls):
            inter = []
            for j in range(nseg):
                rs = slice(j * seg, (j + 1) * seg)
                s_prev = load_s(j, h)
                b_last = bcum[(j + 1) * seg - 1:(j + 1) * seg, sl]
                kd = kk[rs, sl] * jnp.exp2(b_last - bcum[rs, sl])
                inter.append(_dot(qb[rs, sl].astype(seg_dtype), s_prev.astype(seg_dtype)))
                bl = _dot_tn(jnp.concatenate([g1s[rs, sl], g2s[rs, sl]], axis=0), ones)
                store_s(j, h, jnp.exp2(bl) * s_prev + _dot_tn(kd.astype(seg_dtype), v[rs, sl].astype(seg_dtype)))
            o[h] = o[h] + (inter[0] if nseg == 1 else jnp.concatenate(inter, axis=0))
        for h, sl in enumerate(sls):
            ms = jnp.mean(o[h] * o[h], axis=-1, keepdims=True)
            o_ref[pl.ds(r0, c), sl] = (o[h] * lax.rsqrt(ms + EPS) * gn * gate[:, sl]).astype(o_ref.dtype)

    if carry:
        ti = pl.program_id(2)

        @pl.when(ti == 0)
        def _():
            s_scr[...] = s0_ref[0]

        def load_s(j, h):
            return s_scr[h]

        def store_s(j, h, val):
            s_scr[h] = val

        def loop(n, cr):
            chunk(pl.multiple_of(n * c, c), load_s, store_s)
            return cr

        lax.fori_loop(0, n_chunks, loop, 0)

        @pl.when(ti == pl.num_programs(2) - 1)
        def _():
            sout_ref[0] = s_scr[...]
    else:
        for n in range(n_chunks):
            def load_s(j, h, n=n):
                return s0_ref[n * nseg + j, h]

            def store_s(j, h, val, n=n):
                sout_ref[n * nseg + j, h] = val

            chunk(n * c, load_s, store_s)


def _hgrn(z, lb_logits, g_norm, state, bsz, t, d_hgrn, col0):
    dk = HGRN_HEAD_DIM
    nh = d_hgrn // dk
    hb = min(HGRN_HEADS_PER_STEP, nh)
    c = HGRN_CHUNK
    wl = hb * dk
    assert nh % hb == 0 and col0 % wl == 0 and d_hgrn % wl == 0
    if t >= c:
        assert t % c == 0
        carry, seg = True, c
        rows = _pick(t, HGRN_ROWS, c)
        nseq = 1
    else:
        carry, seg = False, t
        rows = c
        nseq = rows // t
        assert bsz % nseq == 0
    n_chunks = rows // c
    nt = (nseq * t) // rows
    nlev = int(math.log2(seg))
    mats, masks = _hgrn_consts(c, seg)
    mats = np.concatenate([mats, mats], axis=1)
    cb = col0 // wl
    kstride = d_hgrn // wl

    def zspec(k):
        return pl.BlockSpec((rows, wl), lambda b, h, i: (b * nt + i, cb + k * kstride + h))

    body = functools.partial(_hgrn_body, hb=hb, n_chunks=n_chunks, c=c, seg=seg, nlev=nlev, carry=carry)
    o, s_out = pl.pallas_call(
        body,
        out_shape=(jax.ShapeDtypeStruct((bsz * t, d_hgrn), BF16),
                   jax.ShapeDtypeStruct((bsz, nh, dk, dk), F32)),
        grid=(bsz // nseq, nh // hb, nt),
        in_specs=[zspec(0), zspec(1), zspec(2), zspec(3),
                  pl.BlockSpec((2, wl), lambda b, h, i: (0, h)),
                  pl.BlockSpec((1, dk), lambda b, h, i: (0, 0)),
                  pl.BlockSpec(mats.shape, lambda b, h, i: (0, 0)),
                  pl.BlockSpec(masks.shape, lambda b, h, i: (0, 0, 0)),
                  pl.BlockSpec((nseq, hb, dk, dk), lambda b, h, i: (b, h, 0, 0))],
        out_specs=(pl.BlockSpec((rows, wl), lambda b, h, i: (b * nt + i, h)),
                   pl.BlockSpec((nseq, hb, dk, dk), lambda b, h, i: (b, h, 0, 0))),
        scratch_shapes=[pltpu.VMEM((hb, dk, dk), F32)],
        compiler_params=_params("parallel", "parallel", "arbitrary"),
        name="hgrn2",
    )(z, z, z, z, lb_logits, g_norm.reshape(1, dk), jnp.asarray(mats, BF16), jnp.asarray(masks), state)
    return o, s_out


def _merge_body(ca_ref, ob_ref, wc_ref, wh_ref, ga_ref, gb_ref, o_ref):
    ya = _dot(ca_ref[...], wc_ref[...])
    yb = _dot(ob_ref[...], wh_ref[...])
    o_ref[...] = (_sigmoid(ga_ref[...]) * ya + _sigmoid(gb_ref[...]) * yb).astype(o_ref.dtype)


def _merge(ca, ob, w_conv_out, w_hgrn_out, z, gate_col0, d_model):
    m, kc = ca.shape
    kh = ob.shape[1]
    tm = _pick(m, 1024, BF16_SUBLANES)
    tn = _pick(d_model, 512, LANES)
    assert gate_col0 % tn == 0
    ga0 = gate_col0 // tn
    gb0 = (gate_col0 + d_model) // tn
    return pl.pallas_call(
        _merge_body,
        out_shape=jax.ShapeDtypeStruct((m, d_model), BF16),
        grid=(m // tm, d_model // tn),
        in_specs=[pl.BlockSpec((tm, kc), lambda i, j: (i, 0)),
                  pl.BlockSpec((tm, kh), lambda i, j: (i, 0)),
                  pl.BlockSpec((kc, tn), lambda i, j: (0, j)),
                  pl.BlockSpec((kh, tn), lambda i, j: (0, j)),
                  pl.BlockSpec((tm, tn), lambda i, j: (i, ga0 + j)),
                  pl.BlockSpec((tm, tn), lambda i, j: (i, gb0 + j))],
        out_specs=pl.BlockSpec((tm, tn), lambda i, j: (i, j)),
        compiler_params=_params("parallel", "parallel"),
        name="mixer_merge",
    )(ca, ob, w_conv_out, w_hgrn_out, z, z)


def _softmax_rows(sc):
    mx = jnp.max(sc, axis=-1, keepdims=True)
    p = jnp.exp(sc - mx)
    return p / jnp.sum(p, axis=-1, keepdims=True)


def _xblock_body(x_ref, g1_ref, wq_ref, k_ref, v_ref, wo_ref, g2_ref, x2_ref, h2_ref,
                 *, nb, tq, nh, head_rows):
    dh = XATTN_HEAD_DIM
    scale = dh ** -0.5
    x = x_ref[...]
    ms = jnp.mean(x * x, axis=-1, keepdims=True)
    h = (x * lax.rsqrt(ms + EPS) * g1_ref[...]).astype(BF16)
    qx = _dot(h, wq_ref[...])
    if head_rows:
        nk = k_ref.shape[1]
        row_head = lax.broadcasted_iota(jnp.int32, (nh * tq, nk), 0) // tq
        col_head = lax.broadcasted_iota(jnp.int32, (nh * tq, nk), 1) % nh
        same_head = row_head == col_head
    outs = []
    for s in range(nb):
        q = qx[s * tq:(s + 1) * tq]
        kmat = k_ref[s].astype(BF16)
        vmat = v_ref[s].astype(BF16)
        if head_rows:
            qs = jnp.concatenate([q[:, hd * dh:(hd + 1) * dh] for hd in range(nh)], axis=0)
            sc = jnp.where(same_head, _dot_nt(qs.astype(BF16), kmat) * scale, -1e30)
            o = _dot(_softmax_rows(sc).astype(BF16), vmat)
            outs.append(jnp.concatenate([o[hd * tq:(hd + 1) * tq] for hd in range(nh)], axis=1))
        else:
            heads = []
            for hd in range(nh):
                sl = slice(hd * dh, (hd + 1) * dh)
                sc = _dot_nt(q[:, sl].astype(BF16), kmat[:, sl]) * scale
                heads.append(_dot(_softmax_rows(sc).astype(BF16), vmat[:, sl]))
            outs.append(jnp.concatenate(heads, axis=1))
    ox = outs[0] if nb == 1 else jnp.concatenate(outs, axis=0)
    x2 = x + _dot(ox.astype(BF16), wo_ref[...])
    x2_ref[...] = x2
    ms2 = jnp.mean(x2 * x2, axis=-1, keepdims=True)
    h2_ref[...] = (x2 * lax.rsqrt(ms2 + EPS) * g2_ref[...]).astype(h2_ref.dtype)


def _xattn_block(x, g_in, w_xq, mem_k, mem_v, w_xo, g_next, bsz, t):
    m, d = x.shape
    dx = w_xq.shape[1]
    nh = dx // XATTN_HEAD_DIM
    nk, kw = mem_k.shape[1:]
    head_rows = kw == XATTN_HEAD_DIM and nh > 1
    if t >= 256:
        nb, tq = 1, 256
    else:
        nb, tq = _pick(bsz, max(1, 64 // t), 1), t
    assert t % tq == 0 and bsz % nb == 0
    assert (nb * tq) % BF16_SUBLANES == 0 and (nh * tq) % BF16_SUBLANES == 0
    nt = t // tq
    rows = nb * tq
    body = functools.partial(_xblock_body, nb=nb, tq=tq, nh=nh, head_rows=head_rows)
    return pl.pallas_call(
        body,
        out_shape=(jax.ShapeDtypeStruct((m, d), F32), jax.ShapeDtypeStruct((m, d), BF16)),
        grid=(bsz // nb, nt),
        in_specs=[pl.BlockSpec((rows, d), lambda b, i: (b * nt + i, 0)),
                  pl.BlockSpec((1, d), lambda b, i: (0, 0)),
                  pl.BlockSpec((d, dx), lambda b, i: (0, 0)),
                  pl.BlockSpec((nb, nk, kw), lambda b, i: (b, 0, 0)),
                  pl.BlockSpec((nb, nk, kw), lambda b, i: (b, 0, 0)),
                  pl.BlockSpec((dx, d), lambda b, i: (0, 0)),
                  pl.BlockSpec((1, d), lambda b, i: (0, 0))],
        out_specs=(pl.BlockSpec((rows, d), lambda b, i: (b * nt + i, 0)),
                   pl.BlockSpec((rows, d), lambda b, i: (b * nt + i, 0))),
        compiler_params=_params("parallel", "parallel"),
        name="xattn_block",
    )(x, g_in.reshape(1, d), w_xq, mem_k, mem_v, w_xo, g_next.reshape(1, d))


def _ffn_up_body(h_ref, wa_ref, wv_ref, dw_ref, db_ref, st_ref, o_ref, nst_ref, *, nseq, rows_per_seq, sub):
    tm = h_ref.shape[0]
    tn = wa_ref.shape[1]
    wa = wa_ref[...].astype(BF16)
    wv = wv_ref[...].astype(BF16)
    w0, w1, w2 = dw_ref[0:1, :], dw_ref[1:2, :], dw_ref[2:3, :]
    bias = db_ref[...]
    if nseq == 1:
        prev = st_ref[0]
        rps = sub
    else:
        rps = rows_per_seq
    nq = sub // rps
    tpos = lax.broadcasted_iota(jnp.int32, (sub, tn), 0) % rps
    for qi in range(tm // sub):
        hq = h_ref[qi * sub:(qi + 1) * sub, :]
        a = _dot(hq, wa)
        val = _dot(hq, wv)
        if nseq == 1:
            p0, p1 = prev[0:1, :], prev[1:2, :]
        else:
            st = st_ref[qi * nq:(qi + 1) * nq]
            p0 = jnp.broadcast_to(st[:, 0:1, :], (nq, rps, tn)).reshape(sub, tn)
            p1 = jnp.broadcast_to(st[:, 1:2, :], (nq, rps, tn)).reshape(sub, tn)
        a1 = jnp.where(tpos == 0, p1, pltpu.roll(a, 1, 0))
        a2 = jnp.where(tpos == 0, p0, jnp.where(tpos == 1, p1, pltpu.roll(a, 2, 0)))
        ac = w0 * a2 + w1 * a1 + w2 * a + bias
        o_ref[qi * sub:(qi + 1) * sub, :] = (ac * _sigmoid(ac) * val).astype(o_ref.dtype)
        if nseq == 1:
            prev = a[sub - 2:sub, :]
        else:
            nst_ref[qi * nq:(qi + 1) * nq] = a.reshape(nq, rps, tn)[:, rps - 2:rps, :]
    if nseq == 1:
        nst_ref[0] = prev


def _ffn_up(h, w_up, w_dw, b_dw, state, bsz, t, d_ff):
    m, d = h.shape
    tm = _pick(m, FFN_TM, BF16_SUBLANES)
    tn = _pick(d_ff, 256, LANES)
    assert tm % t == 0 and (tm == t or t % SUBLANES == 0)
    nseq = tm // t
    nj = d_ff // tn
    n_tiles = m // tm
    sub = _pick(tm, FFN_SUB, BF16_SUBLANES)
    assert tm % sub == 0 and (nseq == 1 or sub % t == 0)
    body = functools.partial(_ffn_up_body, nseq=nseq, rows_per_seq=t, sub=sub)

    hmid, new_state = pl.pallas_call(
        body,
        out_shape=(jax.ShapeDtypeStruct((m, d_ff), BF16),
                   jax.ShapeDtypeStruct((bsz, 2, d_ff), F32)),
        grid=(n_tiles, nj),
        in_specs=[pl.BlockSpec((tm, d), lambda i, j: (i, 0), pipeline_mode=pl.Buffered(1)),
                  pl.BlockSpec((d, tn), lambda i, j: (0, j)),
                  pl.BlockSpec((d, tn), lambda i, j: (0, nj + j)),
                  pl.BlockSpec((3, tn), lambda i, j: (0, j)),
                  pl.BlockSpec((1, tn), lambda i, j: (0, j)),
                  pl.BlockSpec((nseq, 2, tn), lambda i, j: (i, 0, j))],
        out_specs=(pl.BlockSpec((tm, tn), lambda i, j: (i, j)),
                   pl.BlockSpec((nseq, 2, tn), lambda i, j: (i, 0, j))),
        compiler_params=_params("parallel", "parallel"),
        name="ffn_up",
    )(h, w_up, w_up, w_dw, b_dw.reshape(1, d_ff), state)
    return hmid, new_state


def _layer(x, mem_k, mem_v, conv_st, hgrn_st, ffn_st, lb_logits, wts, bsz, t):
    d_model = x.shape[1]
    d_conv = wts["w_conv_out"].shape[0]
    d_hgrn = wts["w_hgrn_out"].shape[0]
    d_ff = wts["w_down"].shape[0]

    h = _rmsnorm(x, wts["g_mix"], BF16)
    z = _matmul(h, wts["w_in"], tm=2048, tn=512, a_buffers=1, name="in_proj")
    ca, new_conv = _conformer_conv(z, conv_st, wts["w_conv_dw"], wts["b_conv_dw"],
                                   wts["ln_conv_g"], wts["ln_conv_b"], bsz, t, d_conv)
    ob, new_hgrn = _hgrn(z, lb_logits, wts["g_hgrn_norm"], hgrn_st, bsz, t, d_hgrn, 2 * d_conv)
    merged = _merge(ca, ob, wts["w_conv_out"], wts["w_hgrn_out"], z, 2 * d_conv + 4 * d_hgrn, d_model)
    x = _matmul(merged, wts["w_mix_out"], res=x, name="mix_out")

    x, h = _xattn_block(x, wts["g_xattn"], wts["w_xq"], mem_k, mem_v, wts["w_xo"], wts["g_ffn"], bsz, t)

    hmid, new_ffn = _ffn_up(h, wts["w_up"], wts["w_ffn_dw"], wts["b_ffn_dw"], ffn_st, bsz, t, d_ff)
    x = _matmul(hmid, wts["w_down"], res=x, tm=512, tn=512, name="ffn_down")
    return x, new_conv, new_hgrn, new_ffn


def kernel(x_prompt, x_sample, mem_prompt, state_conv, state_hgrn, cache_mem_k, cache_mem_v, state_ffn_conv, g_mix, w_in, w_conv_dw, b_conv_dw, ln_conv_g, ln_conv_b, w_conv_out, lb_logits, g_hgrn_norm, w_hgrn_out, w_mix_out, g_xattn, g_mem, w_xq, w_xk, w_xv, w_xo, g_ffn, w_up, w_ffn_dw, b_ffn_dw, w_down, g_final):
    depth = w_in.shape[0]
    assert depth == 1 and lb_logits.shape[0] == depth + 1
    bp, tp, d_model = x_prompt.shape
    bs, ts, _ = x_sample.shape
    n_mem = mem_prompt.shape[1]
    d_x = w_xq.shape[2]
    nhx = d_x // XATTN_HEAD_DIM
    d_conv = w_conv_out.shape[1]
    d_hgrn = w_hgrn_out.shape[1]
    d_ff = w_down.shape[1]
    dk = HGRN_HEAD_DIM
    nh = d_hgrn // dk
    cw = w_conv_dw.shape[1]

    def w2(w):
        return w.reshape(w.shape[1:]).astype(BF16)

    def v1(p):
        return p.reshape(p.shape[1:])

    wts = dict(
        g_mix=v1(g_mix), w_in=v1(w_in), w_conv_dw=v1(w_conv_dw), b_conv_dw=v1(b_conv_dw),
        ln_conv_g=v1(ln_conv_g), ln_conv_b=v1(ln_conv_b), w_conv_out=w2(w_conv_out),
        g_hgrn_norm=v1(g_hgrn_norm), w_hgrn_out=w2(w_hgrn_out), w_mix_out=w2(w_mix_out),
        g_xattn=v1(g_xattn), w_xq=w2(w_xq), w_xo=w2(w_xo), g_ffn=v1(g_ffn), w_up=v1(w_up),
        w_ffn_dw=v1(w_ffn_dw), b_ffn_dw=v1(b_ffn_dw), w_down=w2(w_down))

    mem_h = _rmsnorm(mem_prompt.reshape(bp * n_mem, d_model), v1(g_mem), BF16)
    mk = _matmul(mem_h, w2(w_xk), name="mem_k")
    mv = _matmul(mem_h, w2(w_xv), name="mem_v")
    xp, p_conv, p_hgrn, p_ffn = _layer(
        x_prompt.reshape(bp * tp, d_model), mk.reshape(bp, n_mem, d_x), mv.reshape(bp, n_mem, d_x),
        jnp.zeros((bp, cw - 1, d_conv), F32), jnp.zeros((bp, nh, dk, dk), F32),
        jnp.zeros((bp, 2, d_ff), F32), lb_logits, wts, bp, tp)
    xs, s_conv, s_hgrn, s_ffn = _layer(
        x_sample.reshape(bs * ts, d_model), cache_mem_k.reshape(bs, n_mem * nhx, XATTN_HEAD_DIM),
        cache_mem_v.reshape(bs, n_mem * nhx, XATTN_HEAD_DIM), state_conv.reshape(bs, cw - 1, d_conv),
        state_hgrn.reshape(bs, nh, dk, dk), state_ffn_conv.reshape(bs, 2, d_ff),
        lb_logits, wts, bs, ts)

    y_prompt = _rmsnorm(xp, g_final, F32).reshape(bp, tp, d_model)
    y_sample = _rmsnorm(xs, g_final, F32).reshape(bs, ts, d_model)
    return (y_prompt, y_sample,
            p_conv[None], p_hgrn[None],
            mk.reshape(1, bp, n_mem, nhx, XATTN_HEAD_DIM), mv.reshape(1, bp, n_mem, nhx, XATTN_HEAD_DIM),
            p_ffn[None],
            s_conv[None], s_hgrn[None], s_ffn[None])
```

```python
import functools
import math

import jax
import jax.numpy as jnp
import numpy as np
from jax import lax
from jax.experimental import pallas as pl
from jax.experimental.pallas import tpu as pltpu

F32 = jnp.float32
BF16 = jnp.bfloat16
EPS = 1e-6
LOG2E = math.log2(math.e)

LANES = 128
SUBLANES = 8
BF16_SUBLANES = 16
VMEM_LIMIT_BYTES = 56 * 1024 * 1024

HGRN_HEAD_DIM = 128
XATTN_HEAD_DIM = 128
HGRN_CHUNK = 128
HGRN_HEADS_PER_STEP = 8
HGRN_MXU_LEVELS = 3
HGRN_ROWS = 1024
FFN_TM = 2048
FFN_SUB = 512


def _pick(n, target, mult):
    if n <= target:
        return n
    best = None
    for d in range(mult, target + 1, mult):
        if n % d == 0:
            best = d
    assert best is not None, (n, target, mult)
    return best


def _params(*sem):
    return pltpu.CompilerParams(dimension_semantics=sem, vmem_limit_bytes=VMEM_LIMIT_BYTES)


def _sigmoid(x):
    return 1.0 / (1.0 + jnp.exp(-x))


def _dot(a, b):
    return jnp.dot(a, b, preferred_element_type=F32)


def _dot_nt(a, b):
    return lax.dot_general(a, b, (((1,), (1,)), ((), ())), preferred_element_type=F32)


def _dot_tn(a, b):
    return lax.dot_general(a, b, (((0,), (0,)), ((), ())), preferred_element_type=F32)


def _rmsnorm_body(x_ref, g_ref, o_ref):
    x = x_ref[...]
    ms = jnp.mean(x * x, axis=-1, keepdims=True)
    o_ref[...] = (x * lax.rsqrt(ms + EPS) * g_ref[...]).astype(o_ref.dtype)


def _rmsnorm(x, g, out_dtype):
    m, d = x.shape
    tr = _pick(m, 512, BF16_SUBLANES)
    return pl.pallas_call(
        _rmsnorm_body,
        out_shape=jax.ShapeDtypeStruct((m, d), out_dtype),
        grid=(m // tr,),
        in_specs=[pl.BlockSpec((tr, d), lambda i: (i, 0)),
                  pl.BlockSpec((1, d), lambda i: (0, 0))],
        out_specs=pl.BlockSpec((tr, d), lambda i: (i, 0)),
        compiler_params=_params("parallel"),
        name="rmsnorm",
    )(x, g.reshape(1, d))


def _mm_body(a_ref, w_ref, o_ref):
    o_ref[...] = _dot(a_ref[...], w_ref[...].astype(BF16)).astype(o_ref.dtype)


def _mm_res_body(a_ref, w_ref, r_ref, o_ref):
    o_ref[...] = (r_ref[...] + _dot(a_ref[...], w_ref[...].astype(BF16))).astype(o_ref.dtype)


def _matmul(a, w, res=None, out_dtype=F32, tm=1024, tn=1024, a_buffers=2, name="matmul"):
    m, k = a.shape
    n = w.shape[1]
    tm = _pick(m, tm, BF16_SUBLANES)
    tn = _pick(n, tn, LANES)
    a_mode = {} if a_buffers == 2 else dict(pipeline_mode=pl.Buffered(a_buffers))
    in_specs = [pl.BlockSpec((tm, k), lambda i, j: (i, 0), **a_mode),
                pl.BlockSpec((k, tn), lambda i, j: (0, j))]
    args = [a, w]
    body = _mm_body
    if res is not None:
        in_specs.append(pl.BlockSpec((tm, tn), lambda i, j: (i, j)))
        args.append(res)
        body = _mm_res_body
    return pl.pallas_call(
        body,
        out_shape=jax.ShapeDtypeStruct((m, n), out_dtype),
        grid=(m // tm, n // tn),
        in_specs=in_specs,
        out_specs=pl.BlockSpec((tm, tn), lambda i, j: (i, j)),
        compiler_params=_params("parallel", "parallel"),
        name=name,
    )(*args)


CONV_HALO = 32


def _conv_body(za_ref, zb_ref, st_ref, w_ref, b_ref, lg_ref, lb_ref, ca_ref, nst_ref, uext, ushift, cbuf,
               *, nb, tt, c, width, cch, rch):
    ti = pl.program_id(1)
    h0 = CONV_HALO - (width - 1)
    rows = CONV_HALO + tt
    for s in range(nb):
        @pl.when(ti == 0)
        def _():
            uext[s, h0:CONV_HALO, :] = st_ref[s]

        @pl.when(ti > 0)
        def _():
            uext[s, h0:CONV_HALO, :] = uext[s, h0 + tt:CONV_HALO + tt, :]

        uext[s, CONV_HALO:rows, :] = za_ref[s] * _sigmoid(zb_ref[s])
        for sh in range(1, SUBLANES):
            ushift[sh - 1] = uext[s, sh:sh + rows - SUBLANES, :]
        for c0 in range(0, c, cch):
            for r0 in range(0, tt, rch):
                acc = jnp.broadcast_to(b_ref[:, c0:c0 + cch], (rch, cch))
                for k in range(width):
                    sh = (h0 + k) % SUBLANES
                    lo = h0 + k - sh + r0
                    if sh == 0:
                        win = uext[s, lo:lo + rch, c0:c0 + cch]
                    else:
                        win = ushift[sh - 1, lo:lo + rch, c0:c0 + cch]
                    acc = acc + win * w_ref[k:k + 1, c0:c0 + cch]
                cbuf[s * tt + r0:s * tt + r0 + rch, c0:c0 + cch] = acc
        nst_ref[s] = uext[s, rows - (width - 1):rows, :]
    x = cbuf[...]
    mu = jnp.mean(x, axis=-1, keepdims=True)
    xc = x - mu
    var = jnp.mean(xc * xc, axis=-1, keepdims=True)
    y = xc * lax.rsqrt(var + EPS) * lg_ref[...] + lb_ref[...]
    ca_ref[...] = (y * _sigmoid(y)).astype(ca_ref.dtype)


def _conformer_conv(z, state, w_dw, b_dw, ln_g, ln_b, bsz, t, c):
    width = w_dw.shape[0]
    d_in = z.shape[1]
    z3 = z.reshape(bsz, t, d_in)
    if t >= 128:
        nb, tt = 1, 128
    else:
        nb, tt = _pick(bsz, max(1, 128 // t), 1), t
    assert t % tt == 0 and bsz % nb == 0 and (tt >= width - 1 or tt == t)
    assert (nb * tt) % BF16_SUBLANES == 0 and tt % SUBLANES == 0 and width - 1 <= CONV_HALO
    cch = _pick(c, 512, LANES)
    rch = _pick(tt, 64, SUBLANES)
    body = functools.partial(_conv_body, nb=nb, tt=tt, c=c, width=width, cch=cch, rch=rch)
    nt = t // tt
    ca, nst = pl.pallas_call(
        body,
        out_shape=(jax.ShapeDtypeStruct((bsz * t, c), BF16),
                   jax.ShapeDtypeStruct((bsz, width - 1, c), F32)),
        grid=(bsz // nb, nt),
        in_specs=[pl.BlockSpec((nb, tt, c), lambda b, i: (b, i, 0)),
                  pl.BlockSpec((nb, tt, c), lambda b, i: (b, i, 1)),
                  pl.BlockSpec((nb, width - 1, c), lambda b, i: (b, 0, 0)),
                  pl.BlockSpec((width, c), lambda b, i: (0, 0)),
                  pl.BlockSpec((1, c), lambda b, i: (0, 0)),
                  pl.BlockSpec((1, c), lambda b, i: (0, 0)),
                  pl.BlockSpec((1, c), lambda b, i: (0, 0))],
        out_specs=(pl.BlockSpec((nb * tt, c), lambda b, i: (b * nt + i, 0)),
                   pl.BlockSpec((nb, width - 1, c), lambda b, i: (b, 0, 0))),
        scratch_shapes=[pltpu.VMEM((nb, CONV_HALO + tt, c), F32),
                        pltpu.VMEM((SUBLANES - 1, CONV_HALO + tt - SUBLANES, c), F32),
                        pltpu.VMEM((nb * tt, c), F32)],
        compiler_params=_params("parallel", "arbitrary"),
        name="conformer_conv",
    )(z3, z3, state, w_dw, b_dw.reshape(1, c), ln_g.reshape(1, c), ln_b.reshape(1, c))
    return ca, nst


def _hgrn_consts(c, seg):
    nlev = int(math.log2(seg))
    assert 1 << nlev == seg and c % seg == 0
    t = np.arange(c)
    mxu_lev = min(nlev, HGRN_MXU_LEVELS)
    mats = np.zeros((mxu_lev + 1, c, c), np.float32)
    mats[0] = (t[None, :] <= t[:, None]) & ((t[None, :] // seg) == (t[:, None] // seg))
    masks = np.zeros((nlev, c, c), np.float32)
    for lev in range(1, nlev + 1):
        half = 1 << (lev - 1)
        blk = t >> lev
        upper = ((t >> (lev - 1)) & 1).astype(bool)
        p = (blk << lev) + half - 1
        r = t[None, :]
        if lev <= mxu_lev:
            up_rows = (r > p[:, None]) & (r <= t[:, None])
            lo_rows = (r > t[:, None]) & (r <= p[:, None])
            mats[lev] = np.where(upper[:, None], up_rows, lo_rows)
        masks[lev - 1] = (blk[:, None] == blk[None, :]) & upper[:, None] & (~upper[None, :])
    return mats.reshape((mxu_lev + 1) * c, c), masks


def _level_exponent(bcum, lev):
    half, size = 1 << (lev - 1), 1 << lev
    parts = []
    for b0 in range(0, bcum.shape[0], size):
        ref = bcum[b0 + half - 1:b0 + half, :]
        parts.append(ref - bcum[b0:b0 + half])
        parts.append(bcum[b0 + half:b0 + size] - ref)
    return jnp.concatenate(parts, axis=0)


def _hgrn_body(zq_ref, zf_ref, zi_ref, zg_ref, lbl_ref, gn_ref, mall_ref, masks_ref, s0_ref,
               o_ref, sout_ref, s_scr, *, hb, n_chunks, c, seg, nlev, carry):
    dk = HGRN_HEAD_DIM
    nseg = c // seg
    seg_dtype = BF16 if seg % BF16_SUBLANES == 0 else F32

    l0 = lbl_ref[0:1, :]
    l1 = lbl_ref[1:2, :]
    lmax = jnp.maximum(l0, l1)
    e0 = jnp.exp(l0 - lmax)
    lb = e0 / (e0 + jnp.exp(l1 - lmax))
    gn = gn_ref[...]
    mall = mall_ref[...]
    ones = jnp.ones((2 * seg, dk), seg_dtype)

    def chunk(r0, load_s, store_s):
        zq = zq_ref[pl.ds(r0, c), :]
        zf = zf_ref[pl.ds(r0, c), :]
        v = zi_ref[pl.ds(r0, c), :]
        zg = zg_ref[pl.ds(r0, c), :]
        f = lb + (1.0 - lb) * _sigmoid(zf)
        g = jnp.log(f) * LOG2E
        kk = 1.0 - f
        q = zq * _sigmoid(zq)
        gate = zg * _sigmoid(zg)
        g1 = g.astype(BF16)
        g2 = (g - g1.astype(F32)).astype(BF16)
        e_all = _dot(mall, jnp.concatenate([g1, g2], axis=0))
        g1s, g2s = g1.astype(seg_dtype), g2.astype(seg_dtype)
        sls = [slice(h * dk, (h + 1) * dk) for h in range(hb)]
        bcum = e_all[0:c]
        vm = v.astype(BF16)
        a = [None] * hb
        for lev in range(1, nlev + 1):
            if lev <= HGRN_MXU_LEVELS:
                e = jnp.exp2(e_all[lev * c:(lev + 1) * c])
            else:
                e = jnp.exp2(_level_exponent(bcum, lev))
            qe = (q * e).astype(BF16)
            ke = (kk * e).astype(BF16)
            for h, sl in enumerate(sls):
                term = masks_ref[lev - 1] * _dot_nt(qe[:, sl], ke[:, sl])
                a[h] = term if a[h] is None else a[h] + term
        qk = q * kk
        qb = q * jnp.exp2(bcum)
        o = [_dot(a[h].astype(BF16), vm[:, sl]) + jnp.sum(qk[:, sl], axis=-1, keepdims=True) * v[:, sl]
             for h, sl in enumerate(sls)]
        for h, sl in enumerate(sls):
            inter = []
            for j in range(nseg):
                rs = slice(j * seg, (j + 1) * seg)
                s_prev = load_s(j, h)
                b_last = bcum[(j + 1) * seg - 1:(j + 1) * seg, sl]
                kd = kk[rs, sl] * jnp.exp2(b_last - bcum[rs, sl])
                inter.append(_dot(qb[rs, sl].astype(seg_dtype), s_prev.astype(seg_dtype)))
                bl = _dot_tn(jnp.concatenate([g1s[rs, sl], g2s[rs, sl]], axis=0), ones)
                store_s(j, h, jnp.exp2(bl) * s_prev + _dot_tn(kd.astype(seg_dtype), v[rs, sl].astype(seg_dtype)))
            o[h] = o[h] + (inter[0] if nseg == 1 else jnp.concatenate(inter, axis=0))
        for h, sl in enumerate(sls):
            ms = jnp.mean(o[h] * o[h], axis=-1, keepdims=True)
            o_ref[pl.ds(r0, c), sl] = (o[h] * lax.rsqrt(ms + EPS) * gn * gate[:, sl]).astype(o_ref.dtype)

    if carry:
        ti = pl.program_id(2)

        @pl.when(ti == 0)
        def _():
            s_scr[...] = s0_ref[0]

        def load_s(j, h):
            return s_scr[h]

        def store_s(j, h, val):
            s_scr[h] = val

        def loop(n, cr):
            chunk(pl.multiple_of(n * c, c), load_s, store_s)
            return cr

        lax.fori_loop(0, n_chunks, loop, 0)

        @pl.when(ti == pl.num_programs(2) - 1)
        def _():
            sout_ref[0] = s_scr[...]
    else:
        for n in range(n_chunks):
            def load_s(j, h, n=n):
                return s0_ref[n * nseg + j, h]

            def store_s(j, h, val, n=n):
                sout_ref[n * nseg + j, h] = val

            chunk(n * c, load_s, store_s)


def _hgrn(z, lb_logits, g_norm, state, bsz, t, d_hgrn, col0):
    dk = HGRN_HEAD_DIM
    nh = d_hgrn // dk
    hb = min(HGRN_HEADS_PER_STEP, nh)
    c = HGRN_CHUNK
    wl = hb * dk
    assert nh % hb == 0 and col0 % wl == 0 and d_hgrn % wl == 0
    if t >= c:
        assert t % c == 0
        carry, seg = True, c
        rows = _pick(t, HGRN_ROWS, c)
        nseq = 1
    else:
        carry, seg = False, t
        rows = c
        nseq = rows // t
        assert bsz % nseq == 0
    n_chunks = rows // c
    nt = (nseq * t) // rows
    nlev = int(math.log2(seg))
    mats, masks = _hgrn_consts(c, seg)
    mats = np.concatenate([mats, mats], axis=1)
    cb = col0 // wl
    kstride = d_hgrn // wl

    def zspec(k):
        return pl.BlockSpec((rows, wl), lambda b, h, i: (b * nt + i, cb + k * kstride + h))

    body = functools.partial(_hgrn_body, hb=hb, n_chunks=n_chunks, c=c, seg=seg, nlev=nlev, carry=carry)
    o, s_out = pl.pallas_call(
        body,
        out_shape=(jax.ShapeDtypeStruct((bsz * t, d_hgrn), BF16),
                   jax.ShapeDtypeStruct((bsz, nh, dk, dk), F32)),
        grid=(bsz // nseq, nh // hb, nt),
        in_specs=[zspec(0), zspec(1), zspec(2), zspec(3),
                  pl.BlockSpec((2, wl), lambda b, h, i: (0, h)),
                  pl.BlockSpec((1, dk), lambda b, h, i: (0, 0)),
                  pl.BlockSpec(mats.shape, lambda b, h, i: (0, 0)),
                  pl.BlockSpec(masks.shape, lambda b, h, i: (0, 0, 0)),
                  pl.BlockSpec((nseq, hb, dk, dk), lambda b, h, i: (b, h, 0, 0))],
        out_specs=(pl.BlockSpec((rows, wl), lambda b, h, i: (b * nt + i, h)),
                   pl.BlockSpec((nseq, hb, dk, dk), lambda b, h, i: (b, h, 0, 0))),
        scratch_shapes=[pltpu.VMEM((hb, dk, dk), F32)],
        compiler_params=_params("parallel", "parallel", "arbitrary"),
        name="hgrn2",
    )(z, z, z, z, lb_logits, g_norm.reshape(1, dk), jnp.asarray(mats, BF16), jnp.asarray(masks), state)
    return o, s_out


def _merge_body(ca_ref, ob_ref, wc_ref, wh_ref, ga_ref, gb_ref, o_ref):
    ya = _dot(ca_ref[...], wc_ref[...])
    yb = _dot(ob_ref[...], wh_ref[...])
    o_ref[...] = (_sigmoid(ga_ref[...]) * ya + _sigmoid(gb_ref[...]) * yb).astype(o_ref.dtype)


def _merge(ca, ob, w_conv_out, w_hgrn_out, z, gate_col0, d_model):
    m, kc = ca.shape
    kh = ob.shape[1]
    tm = _pick(m, 1024, BF16_SUBLANES)
    tn = _pick(d_model, 512, LANES)
    assert gate_col0 % tn == 0
    ga0 = gate_col0 // tn
    gb0 = (gate_col0 + d_model) // tn
    return pl.pallas_call(
        _merge_body,
        out_shape=jax.ShapeDtypeStruct((m, d_model), BF16),
        grid=(m // tm, d_model // tn),
        in_specs=[pl.BlockSpec((tm, kc), lambda i, j: (i, 0)),
                  pl.BlockSpec((tm, kh), lambda i, j: (i, 0)),
                  pl.BlockSpec((kc, tn), lambda i, j: (0, j)),
                  pl.BlockSpec((kh, tn), lambda i, j: (0, j)),
                  pl.BlockSpec((tm, tn), lambda i, j: (i, ga0 + j)),
                  pl.BlockSpec((tm, tn), lambda i, j: (i, gb0 + j))],
        out_specs=pl.BlockSpec((tm, tn), lambda i, j: (i, j)),
        compiler_params=_params("parallel", "parallel"),
        name="mixer_merge",
    )(ca, ob, w_conv_out, w_hgrn_out, z, z)


def _softmax_rows(sc):
    mx = jnp.max(sc, axis=-1, keepdims=True)
    p = jnp.exp(sc - mx)
    return p / jnp.sum(p, axis=-1, keepdims=True)


def _xblock_body(x_ref, g1_ref, wq_ref, k_ref, v_ref, wo_ref, g2_ref, x2_ref, h2_ref,
                 *, nb, tq, nh, head_rows):
    dh = XATTN_HEAD_DIM
    scale = dh ** -0.5
    x = x_ref[...]
    ms = jnp.mean(x * x, axis=-1, keepdims=True)
    h = (x * lax.rsqrt(ms + EPS) * g1_ref[...]).astype(BF16)
    qx = _dot(h, wq_ref[...])
    if head_rows:
        nk = k_ref.shape[1]
        row_head = lax.broadcasted_iota(jnp.int32, (nh * tq, nk), 0) // tq
        col_head = lax.broadcasted_iota(jnp.int32, (nh * tq, nk), 1) % nh
        same_head = row_head == col_head
    hsl = [slice(hd * dh, (hd + 1) * dh) for hd in range(nh)]
    if head_rows:
        units = [(s, None) for s in range(nb)]
    else:
        units = [(s, sl) for s in range(nb) for sl in hsl]
    scores = []
    for s, sl in units:
        q = qx[s * tq:(s + 1) * tq]
        if head_rows:
            qs = jnp.concatenate([q[:, hs] for hs in hsl], axis=0).astype(BF16)
            scores.append(jnp.where(same_head, _dot_nt(qs, k_ref[s].astype(BF16)) * scale, -1e30))
        else:
            scores.append(_dot_nt(q[:, sl].astype(BF16), k_ref[s][:, sl].astype(BF16)) * scale)
    probs = [_softmax_rows(sc).astype(BF16) for sc in scores]
    outs = []
    for (s, sl), p in zip(units, probs):
        if head_rows:
            o = _dot(p, v_ref[s].astype(BF16))
            outs.append(jnp.concatenate([o[hd * tq:(hd + 1) * tq] for hd in range(nh)], axis=1))
        else:
            outs.append(_dot(p, v_ref[s][:, sl].astype(BF16)))
    if not head_rows:
        outs = [jnp.concatenate(outs[s * nh:(s + 1) * nh], axis=1) for s in range(nb)]
    ox = outs[0] if nb == 1 else jnp.concatenate(outs, axis=0)
    x2 = x + _dot(ox.astype(BF16), wo_ref[...])
    x2_ref[...] = x2
    ms2 = jnp.mean(x2 * x2, axis=-1, keepdims=True)
    h2_ref[...] = (x2 * lax.rsqrt(ms2 + EPS) * g2_ref[...]).astype(h2_ref.dtype)


def _xattn_block(x, g_in, w_xq, mem_k, mem_v, w_xo, g_next, bsz, t):
    m, d = x.shape
    dx = w_xq.shape[1]
    nh = dx // XATTN_HEAD_DIM
    nk, kw = mem_k.shape[1:]
    head_rows = kw == XATTN_HEAD_DIM and nh > 1
    if t >= 256:
        nb, tq = 1, 256
    else:
        nb, tq = _pick(bsz, max(1, 64 // t), 1), t
    assert t % tq == 0 and bsz % nb == 0
    assert (nb * tq) % BF16_SUBLANES == 0 and (nh * tq) % BF16_SUBLANES == 0
    nt = t // tq
    rows = nb * tq
    body = functools.partial(_xblock_body, nb=nb, tq=tq, nh=nh, head_rows=head_rows)
    return pl.pallas_call(
        body,
        out_shape=(jax.ShapeDtypeStruct((m, d), F32), jax.ShapeDtypeStruct((m, d), BF16)),
        grid=(bsz // nb, nt),
        in_specs=[pl.BlockSpec((rows, d), lambda b, i: (b * nt + i, 0)),
                  pl.BlockSpec((1, d), lambda b, i: (0, 0)),
                  pl.BlockSpec((d, dx), lambda b, i: (0, 0)),
                  pl.BlockSpec((nb, nk, kw), lambda b, i: (b, 0, 0)),
                  pl.BlockSpec((nb, nk, kw), lambda b, i: (b, 0, 0)),
                  pl.BlockSpec((dx, d), lambda b, i: (0, 0)),
                  pl.BlockSpec((1, d), lambda b, i: (0, 0))],
        out_specs=(pl.BlockSpec((rows, d), lambda b, i: (b * nt + i, 0)),
                   pl.BlockSpec((rows, d), lambda b, i: (b * nt + i, 0))),
        compiler_params=_params("parallel", "parallel"),
        name="xattn_block",
    )(x, g_in.reshape(1, d), w_xq, mem_k, mem_v, w_xo, g_next.reshape(1, d))


def _ffn_up_body(h_ref, wa_ref, wv_ref, dw_ref, db_ref, st_ref, o_ref, nst_ref, *, nseq, rows_per_seq, sub):
    tm = h_ref.shape[0]
    tn = wa_ref.shape[1]
    wa = wa_ref[...].astype(BF16)
    wv = wv_ref[...].astype(BF16)
    w0, w1, w2 = dw_ref[0:1, :], dw_ref[1:2, :], dw_ref[2:3, :]
    bias = db_ref[...]
    if nseq == 1:
        prev = st_ref[0]
        rps = sub
    else:
        rps = rows_per_seq
    nq = sub // rps
    tpos = lax.broadcasted_iota(jnp.int32, (sub, tn), 0) % rps
    for qi in range(tm // sub):
        hq = h_ref[qi * sub:(qi + 1) * sub, :]
        a = _dot(hq, wa)
        val = _dot(hq, wv)
        if nseq == 1:
            p0, p1 = prev[0:1, :], prev[1:2, :]
        else:
            st = st_ref[qi * nq:(qi + 1) * nq]
            p0 = jnp.broadcast_to(st[:, 0:1, :], (nq, rps, tn)).reshape(sub, tn)
            p1 = jnp.broadcast_to(st[:, 1:2, :], (nq, rps, tn)).reshape(sub, tn)
        a1 = jnp.where(tpos == 0, p1, pltpu.roll(a, 1, 0))
        a2 = jnp.where(tpos == 0, p0, jnp.where(tpos == 1, p1, pltpu.roll(a, 2, 0)))
        ac = w0 * a2 + w1 * a1 + w2 * a + bias
        o_ref[qi * sub:(qi + 1) * sub, :] = (ac * _sigmoid(ac) * val).astype(o_ref.dtype)
        if nseq == 1:
            prev = a[sub - 2:sub, :]
        else:
            nst_ref[qi * nq:(qi + 1) * nq] = a.reshape(nq, rps, tn)[:, rps - 2:rps, :]
    if nseq == 1:
        nst_ref[0] = prev


def _ffn_up(h, w_up, w_dw, b_dw, state, bsz, t, d_ff):
    m, d = h.shape
    tm = _pick(m, FFN_TM, BF16_SUBLANES)
    tn = _pick(d_ff, 256, LANES)
    assert tm % t == 0 and (tm == t or t % SUBLANES == 0)
    nseq = tm // t
    nj = d_ff // tn
    n_tiles = m // tm
    sub = _pick(tm, FFN_SUB, BF16_SUBLANES)
    assert tm % sub == 0 and (nseq == 1 or sub % t == 0)
    body = functools.partial(_ffn_up_body, nseq=nseq, rows_per_seq=t, sub=sub)

    hmid, new_state = pl.pallas_call(
        body,
        out_shape=(jax.ShapeDtypeStruct((m, d_ff), BF16),
                   jax.ShapeDtypeStruct((bsz, 2, d_ff), F32)),
        grid=(n_tiles, nj),
        in_specs=[pl.BlockSpec((tm, d), lambda i, j: (i, 0), pipeline_mode=pl.Buffered(1)),
                  pl.BlockSpec((d, tn), lambda i, j: (0, j)),
                  pl.BlockSpec((d, tn), lambda i, j: (0, nj + j)),
                  pl.BlockSpec((3, tn), lambda i, j: (0, j)),
                  pl.BlockSpec((1, tn), lambda i, j: (0, j)),
                  pl.BlockSpec((nseq, 2, tn), lambda i, j: (i, 0, j))],
        out_specs=(pl.BlockSpec((tm, tn), lambda i, j: (i, j)),
                   pl.BlockSpec((nseq, 2, tn), lambda i, j: (i, 0, j))),
        compiler_params=_params("parallel", "parallel"),
        name="ffn_up",
    )(h, w_up, w_up, w_dw, b_dw.reshape(1, d_ff), state)
    return hmid, new_state


def _layer(x, mem_k, mem_v, conv_st, hgrn_st, ffn_st, lb_logits, wts, bsz, t):
    d_model = x.shape[1]
    d_conv = wts["w_conv_out"].shape[0]
    d_hgrn = wts["w_hgrn_out"].shape[0]
    d_ff = wts["w_down"].shape[0]

    h = _rmsnorm(x, wts["g_mix"], BF16)
    z = _matmul(h, wts["w_in"], tm=2048, tn=512, a_buffers=1, name="in_proj")
    ca, new_conv = _conformer_conv(z, conv_st, wts["w_conv_dw"], wts["b_conv_dw"],
                                   wts["ln_conv_g"], wts["ln_conv_b"], bsz, t, d_conv)
    ob, new_hgrn = _hgrn(z, lb_logits, wts["g_hgrn_norm"], hgrn_st, bsz, t, d_hgrn, 2 * d_conv)
    merged = _merge(ca, ob, wts["w_conv_out"], wts["w_hgrn_out"], z, 2 * d_conv + 4 * d_hgrn, d_model)
    x = _matmul(merged, wts["w_mix_out"], res=x, name="mix_out")

    x, h = _xattn_block(x, wts["g_xattn"], wts["w_xq"], mem_k, mem_v, wts["w_xo"], wts["g_ffn"], bsz, t)

    hmid, new_ffn = _ffn_up(h, wts["w_up"], wts["w_ffn_dw"], wts["b_ffn_dw"], ffn_st, bsz, t, d_ff)
    x = _matmul(hmid, wts["w_down"], res=x, tm=512, tn=512, name="ffn_down")
    return x, new_conv, new_hgrn, new_ffn


def kernel(x_prompt, x_sample, mem_prompt, state_conv, state_hgrn, cache_mem_k, cache_mem_v, state_ffn_conv, g_mix, w_in, w_conv_dw, b_conv_dw, ln_conv_g, ln_conv_b, w_conv_out, lb_logits, g_hgrn_norm, w_hgrn_out, w_mix_out, g_xattn, g_mem, w_xq, w_xk, w_xv, w_xo, g_ffn, w_up, w_ffn_dw, b_ffn_dw, w_down, g_final):
    depth = w_in.shape[0]
    assert depth == 1 and lb_logits.shape[0] == depth + 1
    bp, tp, d_model = x_prompt.shape
    bs, ts, _ = x_sample.shape
    n_mem = mem_prompt.shape[1]
    d_x = w_xq.shape[2]
    nhx = d_x // XATTN_HEAD_DIM
    d_conv = w_conv_out.shape[1]
    d_hgrn = w_hgrn_out.shape[1]
    d_ff = w_down.shape[1]
    dk = HGRN_HEAD_DIM
    nh = d_hgrn // dk
    cw = w_conv_dw.shape[1]

    def w2(w):
        return w.reshape(w.shape[1:]).astype(BF16)

    def v1(p):
        return p.reshape(p.shape[1:])

    wts = dict(
        g_mix=v1(g_mix), w_in=v1(w_in), w_conv_dw=v1(w_conv_dw), b_conv_dw=v1(b_conv_dw),
        ln_conv_g=v1(ln_conv_g), ln_conv_b=v1(ln_conv_b), w_conv_out=w2(w_conv_out),
        g_hgrn_norm=v1(g_hgrn_norm), w_hgrn_out=w2(w_hgrn_out), w_mix_out=w2(w_mix_out),
        g_xattn=v1(g_xattn), w_xq=w2(w_xq), w_xo=w2(w_xo), g_ffn=v1(g_ffn), w_up=v1(w_up),
        w_ffn_dw=v1(w_ffn_dw), b_ffn_dw=v1(b_ffn_dw), w_down=w2(w_down))

    mem_h = _rmsnorm(mem_prompt.reshape(bp * n_mem, d_model), v1(g_mem), BF16)
    mk = _matmul(mem_h, w2(w_xk), name="mem_k")
    mv = _matmul(mem_h, w2(w_xv), name="mem_v")
    xp, p_conv, p_hgrn, p_ffn = _layer(
        x_prompt.reshape(bp * tp, d_model), mk.reshape(bp, n_mem, d_x), mv.reshape(bp, n_mem, d_x),
        jnp.zeros((bp, cw - 1, d_conv), F32), jnp.zeros((bp, nh, dk, dk), F32),
        jnp.zeros((bp, 2, d_ff), F32), lb_logits, wts, bp, tp)
    xs, s_conv, s_hgrn, s_ffn = _layer(
        x_sample.reshape(bs * ts, d_model), cache_mem_k.reshape(bs, n_mem * nhx, XATTN_HEAD_DIM),
        cache_mem_v.reshape(bs, n_mem * nhx, XATTN_HEAD_DIM), state_conv.reshape(bs, cw - 1, d_conv),
        state_hgrn.reshape(bs, nh, dk, dk), state_ffn_conv.reshape(bs, 2, d_ff),
        lb_logits, wts, bs, ts)

    y_prompt = _rmsnorm(xp, g_final, F32).reshape(bp, tp, d_model)
    y_sample = _rmsnorm(xs, g_final, F32).reshape(bs, ts, d_model)
    return (y_prompt, y_sample,
            p_conv[None], p_hgrn[None],
            mk.reshape(1, bp, n_mem, nhx, XATTN_HEAD_DIM), mv.reshape(1, bp, n_mem, nhx, XATTN_HEAD_DIM),
            p_ffn[None],
            s_conv[None], s_hgrn[None], s_ffn[None])
```

```python
import functools
import math

import jax
import jax.numpy as jnp
import numpy as np
from jax import lax
from jax.experimental import pallas as pl
from jax.experimental.pallas import tpu as pltpu

F32 = jnp.float32
BF16 = jnp.bfloat16
EPS = 1e-6
LOG2E = math.log2(math.e)

LANES = 128
SUBLANES = 8
BF16_SUBLANES = 16
VMEM_LIMIT_BYTES = 56 * 1024 * 1024

HGRN_HEAD_DIM = 128
XATTN_HEAD_DIM = 128
HGRN_CHUNK = 128
HGRN_HEADS_PER_STEP = 8
HGRN_MXU_LEVELS = 3
HGRN_ROWS = 1024
FFN_TM = 2048
FFN_SUB = 512


def _pick(n, target, mult):
    if n <= target:
        return n
    best = None
    for d in range(mult, target + 1, mult):
        if n % d == 0:
            best = d
    assert best is not None, (n, target, mult)
    return best


def _params(*sem):
    return pltpu.CompilerParams(dimension_semantics=sem, vmem_limit_bytes=VMEM_LIMIT_BYTES)


def _sigmoid(x):
    return 1.0 / (1.0 + jnp.exp(-x))


def _dot(a, b):
    return jnp.dot(a, b, preferred_element_type=F32)


def _dot_nt(a, b):
    return lax.dot_general(a, b, (((1,), (1,)), ((), ())), preferred_element_type=F32)


def _dot_tn(a, b):
    return lax.dot_general(a, b, (((0,), (0,)), ((), ())), preferred_element_type=F32)


def _rmsnorm_body(x_ref, g_ref, o_ref):
    x = x_ref[...]
    ms = jnp.mean(x * x, axis=-1, keepdims=True)
    o_ref[...] = (x * lax.rsqrt(ms + EPS) * g_ref[...]).astype(o_ref.dtype)


def _rmsnorm(x, g, out_dtype):
    m, d = x.shape
    tr = _pick(m, 512, BF16_SUBLANES)
    return pl.pallas_call(
        _rmsnorm_body,
        out_shape=jax.ShapeDtypeStruct((m, d), out_dtype),
        grid=(m // tr,),
        in_specs=[pl.BlockSpec((tr, d), lambda i: (i, 0)),
                  pl.BlockSpec((1, d), lambda i: (0, 0))],
        out_specs=pl.BlockSpec((tr, d), lambda i: (i, 0)),
        compiler_params=_params("parallel"),
        name="rmsnorm",
    )(x, g.reshape(1, d))


def _mm_body(a_ref, w_ref, o_ref):
    o_ref[...] = _dot(a_ref[...], w_ref[...].astype(BF16)).astype(o_ref.dtype)


def _mm_res_body(a_ref, w_ref, r_ref, o_ref):
    o_ref[...] = (r_ref[...] + _dot(a_ref[...], w_ref[...].astype(BF16))).astype(o_ref.dtype)


def _matmul(a, w, res=None, out_dtype=F32, tm=1024, tn=1024, a_buffers=2, name="matmul"):
    m, k = a.shape
    n = w.shape[1]
    tm = _pick(m, tm, BF16_SUBLANES)
    tn = _pick(n, tn, LANES)
    a_mode = {} if a_buffers == 2 else dict(pipeline_mode=pl.Buffered(a_buffers))
    in_specs = [pl.BlockSpec((tm, k), lambda i, j: (i, 0), **a_mode),
                pl.BlockSpec((k, tn), lambda i, j: (0, j))]
    args = [a, w]
    body = _mm_body
    if res is not None:
        in_specs.append(pl.BlockSpec((tm, tn), lambda i, j: (i, j)))
        args.append(res)
        body = _mm_res_body
    return pl.pallas_call(
        body,
        out_shape=jax.ShapeDtypeStruct((m, n), out_dtype),
        grid=(m // tm, n // tn),
        in_specs=in_specs,
        out_specs=pl.BlockSpec((tm, tn), lambda i, j: (i, j)),
        compiler_params=_params("parallel", "parallel"),
        name=name,
    )(*args)


CONV_HALO = 32


def _conv_body(za_ref, zb_ref, st_ref, w_ref, b_ref, lg_ref, lb_ref, ca_ref, nst_ref, uext, ushift, cbuf,
               *, nb, tt, c, width, cch, rch):
    ti = pl.program_id(1)
    h0 = CONV_HALO - (width - 1)
    rows = CONV_HALO + tt
    for s in range(nb):
        @pl.when(ti == 0)
        def _():
            uext[s, h0:CONV_HALO, :] = st_ref[s]

        @pl.when(ti > 0)
        def _():
            uext[s, h0:CONV_HALO, :] = uext[s, h0 + tt:CONV_HALO + tt, :]

        uext[s, CONV_HALO:rows, :] = za_ref[s] * _sigmoid(zb_ref[s])
        for sh in range(1, SUBLANES):
            ushift[sh - 1] = uext[s, sh:sh + rows - SUBLANES, :]
        for c0 in range(0, c, cch):
            for r0 in range(0, tt, rch):
                acc = jnp.broadcast_to(b_ref[:, c0:c0 + cch], (rch, cch))
                for k in range(width):
                    sh = (h0 + k) % SUBLANES
                    lo = h0 + k - sh + r0
                    if sh == 0:
                        win = uext[s, lo:lo + rch, c0:c0 + cch]
                    else:
                        win = ushift[sh - 1, lo:lo + rch, c0:c0 + cch]
                    acc = acc + win * w_ref[k:k + 1, c0:c0 + cch]
                cbuf[s * tt + r0:s * tt + r0 + rch, c0:c0 + cch] = acc
        nst_ref[s] = uext[s, rows - (width - 1):rows, :]
    x = cbuf[...]
    mu = jnp.mean(x, axis=-1, keepdims=True)
    xc = x - mu
    var = jnp.mean(xc * xc, axis=-1, keepdims=True)
    y = xc * lax.rsqrt(var + EPS) * lg_ref[...] + lb_ref[...]
    ca_ref[...] = (y * _sigmoid(y)).astype(ca_ref.dtype)


def _conformer_conv(z, state, w_dw, b_dw, ln_g, ln_b, bsz, t, c):
    width = w_dw.shape[0]
    d_in = z.shape[1]
    z3 = z.reshape(bsz, t, d_in)
    if t >= 128:
        nb, tt = 1, 128
    else:
        nb, tt = _pick(bsz, max(1, 128 // t), 1), t
    assert t % tt == 0 and bsz % nb == 0 and (tt >= width - 1 or tt == t)
    assert (nb * tt) % BF16_SUBLANES == 0 and tt % SUBLANES == 0 and width - 1 <= CONV_HALO
    cch = _pick(c, 512, LANES)
    rch = _pick(tt, 64, SUBLANES)
    body = functools.partial(_conv_body, nb=nb, tt=tt, c=c, width=width, cch=cch, rch=rch)
    nt = t // tt
    ca, nst = pl.pallas_call(
        body,
        out_shape=(jax.ShapeDtypeStruct((bsz * t, c), BF16),
                   jax.ShapeDtypeStruct((bsz, width - 1, c), F32)),
        grid=(bsz // nb, nt),
        in_specs=[pl.BlockSpec((nb, tt, c), lambda b, i: (b, i, 0)),
                  pl.BlockSpec((nb, tt, c), lambda b, i: (b, i, 1)),
                  pl.BlockSpec((nb, width - 1, c), lambda b, i: (b, 0, 0)),
                  pl.BlockSpec((width, c), lambda b, i: (0, 0)),
                  pl.BlockSpec((1, c), lambda b, i: (0, 0)),
                  pl.BlockSpec((1, c), lambda b, i: (0, 0)),
                  pl.BlockSpec((1, c), lambda b, i: (0, 0))],
        out_specs=(pl.BlockSpec((nb * tt, c), lambda b, i: (b * nt + i, 0)),
                   pl.BlockSpec((nb, width - 1, c), lambda b, i: (b, 0, 0))),
        scratch_shapes=[pltpu.VMEM((nb, CONV_HALO + tt, c), F32),
                        pltpu.VMEM((SUBLANES - 1, CONV_HALO + tt - SUBLANES, c), F32),
                        pltpu.VMEM((nb * tt, c), F32)],
        compiler_params=_params("parallel", "arbitrary"),
        name="conformer_conv",
    )(z3, z3, state, w_dw, b_dw.reshape(1, c), ln_g.reshape(1, c), ln_b.reshape(1, c))
    return ca, nst


def _hgrn_consts(c, seg):
    nlev = int(math.log2(seg))
    assert 1 << nlev == seg and c % seg == 0
    t = np.arange(c)
    mxu_lev = min(nlev, HGRN_MXU_LEVELS)
    mats = np.zeros((mxu_lev + 1, c, c), np.float32)
    mats[0] = (t[None, :] <= t[:, None]) & ((t[None, :] // seg) == (t[:, None] // seg))
    masks = np.zeros((nlev, c, c), np.float32)
    for lev in range(1, nlev + 1):
        half = 1 << (lev - 1)
        blk = t >> lev
        upper = ((t >> (lev - 1)) & 1).astype(bool)
        p = (blk << lev) + half - 1
        r = t[None, :]
        if lev <= mxu_lev:
            up_rows = (r > p[:, None]) & (r <= t[:, None])
            lo_rows = (r > t[:, None]) & (r <= p[:, None])
            mats[lev] = np.where(upper[:, None], up_rows, lo_rows)
        masks[lev - 1] = (blk[:, None] == blk[None, :]) & upper[:, None] & (~upper[None, :])
    return mats.reshape((mxu_lev + 1) * c, c), masks


def _level_exponent(bcum, lev):
    half, size = 1 << (lev - 1), 1 << lev
    parts = []
    for b0 in range(0, bcum.shape[0], size):
        ref = bcum[b0 + half - 1:b0 + half, :]
        parts.append(ref - bcum[b0:b0 + half])
        parts.append(bcum[b0 + half:b0 + size] - ref)
    return jnp.concatenate(parts, axis=0)


def _hgrn_body(zq_ref, zf_ref, zi_ref, zg_ref, lbl_ref, gn_ref, mall_ref, masks_ref, s0_ref,
               o_ref, sout_ref, s_scr, *, hb, n_chunks, c, seg, nlev, carry):
    dk = HGRN_HEAD_DIM
    nseg = c // seg
    seg_dtype = BF16 if seg % BF16_SUBLANES == 0 else F32

    l0 = lbl_ref[0:1, :]
    l1 = lbl_ref[1:2, :]
    lmax = jnp.maximum(l0, l1)
    e0 = jnp.exp(l0 - lmax)
    lb = e0 / (e0 + jnp.exp(l1 - lmax))
    gn = gn_ref[...]
    mall = mall_ref[...]
    ones = jnp.ones((2 * seg, dk), seg_dtype)

    def chunk(r0, load_s, store_s):
        zq = zq_ref[pl.ds(r0, c), :]
        zf = zf_ref[pl.ds(r0, c), :]
        v = zi_ref[pl.ds(r0, c), :]
        zg = zg_ref[pl.ds(r0, c), :]
        f = lb + (1.0 - lb) * _sigmoid(zf)
        g = jnp.log(f) * LOG2E
        kk = 1.0 - f
        q = zq * _sigmoid(zq)
        gate = zg * _sigmoid(zg)
        g1 = g.astype(BF16)
        g2 = (g - g1.astype(F32)).astype(BF16)
        e_all = _dot(mall, jnp.concatenate([g1, g2], axis=0))
        g1s, g2s = g1.astype(seg_dtype), g2.astype(seg_dtype)
        sls = [slice(h * dk, (h + 1) * dk) for h in range(hb)]
        bcum = e_all[0:c]
        vm = v.astype(BF16)
        a = [None] * hb
        for lev in range(1, nlev + 1):
            if lev <= HGRN_MXU_LEVELS:
                e = jnp.exp2(e_all[lev * c:(lev + 1) * c])
            else:
                e = jnp.exp2(_level_exponent(bcum, lev))
            qe = (q * e).astype(BF16)
            ke = (kk * e).astype(BF16)
            for h, sl in enumerate(sls):
                term = masks_ref[lev - 1] * _dot_nt(qe[:, sl], ke[:, sl])
                a[h] = term if a[h] is None else a[h] + term
        qk = q * kk
        qb = q * jnp.exp2(bcum)
        o = [_dot(a[h].astype(BF16), vm[:, sl]) + jnp.sum(qk[:, sl], axis=-1, keepdims=True) * v[:, sl]
             for h, sl in enumerate(sls)]
        for h, sl in enumerate(sls):
            inter = []
            for j in range(nseg):
                rs = slice(j * seg, (j + 1) * seg)
                s_prev = load_s(j, h)
                b_last = bcum[(j + 1) * seg - 1:(j + 1) * seg, sl]
                kd = kk[rs, sl] * jnp.exp2(b_last - bcum[rs, sl])
                inter.append(_dot(qb[rs, sl].astype(seg_dtype), s_prev.astype(seg_dtype)))
                bl = _dot_tn(jnp.concatenate([g1s[rs, sl], g2s[rs, sl]], axis=0), ones)
                store_s(j, h, jnp.exp2(bl) * s_prev + _dot_tn(kd.astype(seg_dtype), v[rs, sl].astype(seg_dtype)))
            o[h] = o[h] + (inter[0] if nseg == 1 else jnp.concatenate(inter, axis=0))
        for h, sl in enumerate(sls):
            ms = jnp.mean(o[h] * o[h], axis=-1, keepdims=True)
            o_ref[pl.ds(r0, c), sl] = (o[h] * lax.rsqrt(ms + EPS) * gn * gate[:, sl]).astype(o_ref.dtype)

    if carry:
        ti = pl.program_id(2)

        @pl.when(ti == 0)
        def _():
            s_scr[...] = s0_ref[0]

        def load_s(j, h):
            return s_scr[h]

        def store_s(j, h, val):
            s_scr[h] = val

        def loop(n, cr):
            chunk(pl.multiple_of(n * c, c), load_s, store_s)
            return cr

        lax.fori_loop(0, n_chunks, loop, 0)

        @pl.when(ti == pl.num_programs(2) - 1)
        def _():
            sout_ref[0] = s_scr[...]
    else:
        for n in range(n_chunks):
            def load_s(j, h, n=n):
                return s0_ref[n * nseg + j, h]

            def store_s(j, h, val, n=n):
                sout_ref[n * nseg + j, h] = val

            chunk(n * c, load_s, store_s)


def _hgrn(z, lb_logits, g_norm, state, bsz, t, d_hgrn, col0):
    dk = HGRN_HEAD_DIM
    nh = d_hgrn // dk
    hb = min(HGRN_HEADS_PER_STEP, nh)
    c = HGRN_CHUNK
    wl = hb * dk
    assert nh % hb == 0 and col0 % wl == 0 and d_hgrn % wl == 0
    if t >= c:
        assert t % c == 0
        carry, seg = True, c
        rows = _pick(t, HGRN_ROWS, c)
        nseq = 1
    else:
        carry, seg = False, t
        rows = c
        nseq = rows // t
        assert bsz % nseq == 0
    n_chunks = rows // c
    nt = (nseq * t) // rows
    nlev = int(math.log2(seg))
    mats, masks = _hgrn_consts(c, seg)
    mats = np.concatenate([mats, mats], axis=1)
    cb = col0 // wl
    kstride = d_hgrn // wl

    def zspec(k):
        return pl.BlockSpec((rows, wl), lambda b, h, i: (b * nt + i, cb + k * kstride + h))

    body = functools.partial(_hgrn_body, hb=hb, n_chunks=n_chunks, c=c, seg=seg, nlev=nlev, carry=carry)
    o, s_out = pl.pallas_call(
        body,
        out_shape=(jax.ShapeDtypeStruct((bsz * t, d_hgrn), BF16),
                   jax.ShapeDtypeStruct((bsz, nh, dk, dk), F32)),
        grid=(bsz // nseq, nh // hb, nt),
        in_specs=[zspec(0), zspec(1), zspec(2), zspec(3),
                  pl.BlockSpec((2, wl), lambda b, h, i: (0, h)),
                  pl.BlockSpec((1, dk), lambda b, h, i: (0, 0)),
                  pl.BlockSpec(mats.shape, lambda b, h, i: (0, 0)),
                  pl.BlockSpec(masks.shape, lambda b, h, i: (0, 0, 0)),
                  pl.BlockSpec((nseq, hb, dk, dk), lambda b, h, i: (b, h, 0, 0))],
        out_specs=(pl.BlockSpec((rows, wl), lambda b, h, i: (b * nt + i, h)),
                   pl.BlockSpec((nseq, hb, dk, dk), lambda b, h, i: (b, h, 0, 0))),
        scratch_shapes=[pltpu.VMEM((hb, dk, dk), F32)],
        compiler_params=_params("parallel", "parallel", "arbitrary"),
        name="hgrn2",
    )(z, z, z, z, lb_logits, g_norm.reshape(1, dk), jnp.asarray(mats, BF16), jnp.asarray(masks), state)
    return o, s_out


def _merge_mix_body(ca_ref, ob_ref, wc_ref, wh_ref, ga_ref, gb_ref, wm_ref, x_ref, o_ref, merged, *, n1):
    j = pl.program_id(1)
    tn = wc_ref.shape[1]

    @pl.when(j < n1)
    def _():
        ya = _dot(ca_ref[...], wc_ref[...])
        yb = _dot(ob_ref[...], wh_ref[...])
        blk = (_sigmoid(ga_ref[...]) * ya + _sigmoid(gb_ref[...]) * yb).astype(merged.dtype)
        for jj in range(n1):
            @pl.when(j == jj)
            def _(jj=jj):
                merged[:, jj * tn:(jj + 1) * tn] = blk

    @pl.when(j >= n1)
    def _():
        o_ref[...] = x_ref[...] + _dot(merged[...], wm_ref[...])


def _merge_mix(ca, ob, w_conv_out, w_hgrn_out, z, gate_col0, w_mix, x):
    m, kc = ca.shape
    kh = ob.shape[1]
    d_model = w_mix.shape[1]
    tm = _pick(m, 1024, BF16_SUBLANES)
    tn = _pick(d_model, 512, LANES)
    assert gate_col0 % tn == 0 and w_mix.shape[0] == d_model
    n1 = d_model // tn
    ga0 = gate_col0 // tn
    gb0 = (gate_col0 + d_model) // tn

    def c1(j):
        return jnp.minimum(j, n1 - 1)

    def c2(j):
        return jnp.maximum(j - n1, 0)

    return pl.pallas_call(
        functools.partial(_merge_mix_body, n1=n1),
        out_shape=jax.ShapeDtypeStruct((m, d_model), F32),
        grid=(m // tm, 2 * n1),
        in_specs=[pl.BlockSpec((tm, kc), lambda i, j: (i, 0), pipeline_mode=pl.Buffered(1)),
                  pl.BlockSpec((tm, kh), lambda i, j: (i, 0), pipeline_mode=pl.Buffered(1)),
                  pl.BlockSpec((kc, tn), lambda i, j: (0, c1(j))),
                  pl.BlockSpec((kh, tn), lambda i, j: (0, c1(j))),
                  pl.BlockSpec((tm, tn), lambda i, j: (i, ga0 + c1(j))),
                  pl.BlockSpec((tm, tn), lambda i, j: (i, gb0 + c1(j))),
                  pl.BlockSpec((d_model, tn), lambda i, j: (0, c2(j))),
                  pl.BlockSpec((tm, tn), lambda i, j: (i, c2(j)))],
        out_specs=pl.BlockSpec((tm, tn), lambda i, j: (i, c2(j))),
        scratch_shapes=[pltpu.VMEM((tm, d_model), BF16)],
        compiler_params=_params("parallel", "arbitrary"),
        name="merge_mix",
    )(ca, ob, w_conv_out, w_hgrn_out, z, z, w_mix, x)


def _softmax_rows(sc):
    mx = jnp.max(sc, axis=-1, keepdims=True)
    p = jnp.exp(sc - mx)
    return p / jnp.sum(p, axis=-1, keepdims=True)


def _xblock_body(x_ref, g1_ref, wq_ref, k_ref, v_ref, wo_ref, g2_ref, x2_ref, h2_ref,
                 *, nb, tq, nh, head_rows):
    dh = XATTN_HEAD_DIM
    scale = dh ** -0.5
    x = x_ref[...]
    ms = jnp.mean(x * x, axis=-1, keepdims=True)
    h = (x * lax.rsqrt(ms + EPS) * g1_ref[...]).astype(BF16)
    qx = _dot(h, wq_ref[...])
    if head_rows:
        nk = k_ref.shape[1]
        row_head = lax.broadcasted_iota(jnp.int32, (nh * tq, nk), 0) // tq
        col_head = lax.broadcasted_iota(jnp.int32, (nh * tq, nk), 1) % nh
        same_head = row_head == col_head
    hsl = [slice(hd * dh, (hd + 1) * dh) for hd in range(nh)]
    if head_rows:
        units = [(s, None) for s in range(nb)]
    else:
        units = [(s, sl) for s in range(nb) for sl in hsl]
    scores = []
    for s, sl in units:
        q = qx[s * tq:(s + 1) * tq]
        if head_rows:
            qs = jnp.concatenate([q[:, hs] for hs in hsl], axis=0).astype(BF16)
            scores.append(jnp.where(same_head, _dot_nt(qs, k_ref[s].astype(BF16)) * scale, -1e30))
        else:
            scores.append(_dot_nt(q[:, sl].astype(BF16), k_ref[s][:, sl].astype(BF16)) * scale)
    probs = [_softmax_rows(sc).astype(BF16) for sc in scores]
    outs = []
    for (s, sl), p in zip(units, probs):
        if head_rows:
            o = _dot(p, v_ref[s].astype(BF16))
            outs.append(jnp.concatenate([o[hd * tq:(hd + 1) * tq] for hd in range(nh)], axis=1))
        else:
            outs.append(_dot(p, v_ref[s][:, sl].astype(BF16)))
    if not head_rows:
        outs = [jnp.concatenate(outs[s * nh:(s + 1) * nh], axis=1) for s in range(nb)]
    ox = outs[0] if nb == 1 else jnp.concatenate(outs, axis=0)
    x2 = x + _dot(ox.astype(BF16), wo_ref[...])
    x2_ref[...] = x2
    ms2 = jnp.mean(x2 * x2, axis=-1, keepdims=True)
    h2_ref[...] = (x2 * lax.rsqrt(ms2 + EPS) * g2_ref[...]).astype(h2_ref.dtype)


def _xattn_block(x, g_in, w_xq, mem_k, mem_v, w_xo, g_next, bsz, t):
    m, d = x.shape
    dx = w_xq.shape[1]
    nh = dx // XATTN_HEAD_DIM
    nk, kw = mem_k.shape[1:]
    head_rows = kw == XATTN_HEAD_DIM and nh > 1
    if t >= 256:
        nb, tq = 1, 256
    else:
        nb, tq = _pick(bsz, max(1, 64 // t), 1), t
    assert t % tq == 0 and bsz % nb == 0
    assert (nb * tq) % BF16_SUBLANES == 0 and (nh * tq) % BF16_SUBLANES == 0
    nt = t // tq
    rows = nb * tq
    body = functools.partial(_xblock_body, nb=nb, tq=tq, nh=nh, head_rows=head_rows)
    return pl.pallas_call(
        body,
        out_shape=(jax.ShapeDtypeStruct((m, d), F32), jax.ShapeDtypeStruct((m, d), BF16)),
        grid=(bsz // nb, nt),
        in_specs=[pl.BlockSpec((rows, d), lambda b, i: (b * nt + i, 0)),
                  pl.BlockSpec((1, d), lambda b, i: (0, 0)),
                  pl.BlockSpec((d, dx), lambda b, i: (0, 0)),
                  pl.BlockSpec((nb, nk, kw), lambda b, i: (b, 0, 0)),
                  pl.BlockSpec((nb, nk, kw), lambda b, i: (b, 0, 0)),
                  pl.BlockSpec((dx, d), lambda b, i: (0, 0)),
                  pl.BlockSpec((1, d), lambda b, i: (0, 0))],
        out_specs=(pl.BlockSpec((rows, d), lambda b, i: (b * nt + i, 0)),
                   pl.BlockSpec((rows, d), lambda b, i: (b * nt + i, 0))),
        compiler_params=_params("parallel", "parallel"),
        name="xattn_block",
    )(x, g_in.reshape(1, d), w_xq, mem_k, mem_v, w_xo, g_next.reshape(1, d))


def _ffn_up_body(h_ref, wa_ref, wv_ref, dw_ref, db_ref, st_ref, o_ref, nst_ref, *, nseq, rows_per_seq, sub):
    tm = h_ref.shape[0]
    tn = wa_ref.shape[1]
    wa = wa_ref[...].astype(BF16)
    wv = wv_ref[...].astype(BF16)
    w0, w1, w2 = dw_ref[0:1, :], dw_ref[1:2, :], dw_ref[2:3, :]
    bias = db_ref[...]
    if nseq == 1:
        prev = st_ref[0]
        rps = sub
    else:
        rps = rows_per_seq
    nq = sub // rps
    tpos = lax.broadcasted_iota(jnp.int32, (sub, tn), 0) % rps
    for qi in range(tm // sub):
        hq = h_ref[qi * sub:(qi + 1) * sub, :]
        a = _dot(hq, wa)
        val = _dot(hq, wv)
        if nseq == 1:
            p0, p1 = prev[0:1, :], prev[1:2, :]
        else:
            st = st_ref[qi * nq:(qi + 1) * nq]
            p0 = jnp.broadcast_to(st[:, 0:1, :], (nq, rps, tn)).reshape(sub, tn)
            p1 = jnp.broadcast_to(st[:, 1:2, :], (nq, rps, tn)).reshape(sub, tn)
        a1 = jnp.where(tpos == 0, p1, pltpu.roll(a, 1, 0))
        a2 = jnp.where(tpos == 0, p0, jnp.where(tpos == 1, p1, pltpu.roll(a, 2, 0)))
        ac = w0 * a2 + w1 * a1 + w2 * a + bias
        o_ref[qi * sub:(qi + 1) * sub, :] = (ac * _sigmoid(ac) * val).astype(o_ref.dtype)
        if nseq == 1:
            prev = a[sub - 2:sub, :]
        else:
            nst_ref[qi * nq:(qi + 1) * nq] = a.reshape(nq, rps, tn)[:, rps - 2:rps, :]
    if nseq == 1:
        nst_ref[0] = prev


def _ffn_up(h, w_up, w_dw, b_dw, state, bsz, t, d_ff):
    m, d = h.shape
    tm = _pick(m, FFN_TM, BF16_SUBLANES)
    tn = _pick(d_ff, 256, LANES)
    assert tm % t == 0 and (tm == t or t % SUBLANES == 0)
    nseq = tm // t
    nj = d_ff // tn
    n_tiles = m // tm
    sub = _pick(tm, FFN_SUB, BF16_SUBLANES)
    assert tm % sub == 0 and (nseq == 1 or sub % t == 0)
    body = functools.partial(_ffn_up_body, nseq=nseq, rows_per_seq=t, sub=sub)

    hmid, new_state = pl.pallas_call(
        body,
        out_shape=(jax.ShapeDtypeStruct((m, d_ff), BF16),
                   jax.ShapeDtypeStruct((bsz, 2, d_ff), F32)),
        grid=(n_tiles, nj),
        in_specs=[pl.BlockSpec((tm, d), lambda i, j: (i, 0), pipeline_mode=pl.Buffered(1)),
                  pl.BlockSpec((d, tn), lambda i, j: (0, j)),
                  pl.BlockSpec((d, tn), lambda i, j: (0, nj + j)),
                  pl.BlockSpec((3, tn), lambda i, j: (0, j)),
                  pl.BlockSpec((1, tn), lambda i, j: (0, j)),
                  pl.BlockSpec((nseq, 2, tn), lambda i, j: (i, 0, j))],
        out_specs=(pl.BlockSpec((tm, tn), lambda i, j: (i, j)),
                   pl.BlockSpec((nseq, 2, tn), lambda i, j: (i, 0, j))),
        compiler_params=_params("parallel", "parallel"),
        name="ffn_up",
    )(h, w_up, w_up, w_dw, b_dw.reshape(1, d_ff), state)
    return hmid, new_state


def _layer(x, mem_k, mem_v, conv_st, hgrn_st, ffn_st, lb_logits, wts, bsz, t):
    d_model = x.shape[1]
    d_conv = wts["w_conv_out"].shape[0]
    d_hgrn = wts["w_hgrn_out"].shape[0]
    d_ff = wts["w_down"].shape[0]

    h = _rmsnorm(x, wts["g_mix"], BF16)
    z = _matmul(h, wts["w_in"], tm=2048, tn=512, a_buffers=1, name="in_proj")
    ca, new_conv = _conformer_conv(z, conv_st, wts["w_conv_dw"], wts["b_conv_dw"],
                                   wts["ln_conv_g"], wts["ln_conv_b"], bsz, t, d_conv)
    ob, new_hgrn = _hgrn(z, lb_logits, wts["g_hgrn_norm"], hgrn_st, bsz, t, d_hgrn, 2 * d_conv)
    x = _merge_mix(ca, ob, wts["w_conv_out"], wts["w_hgrn_out"], z, 2 * d_conv + 4 * d_hgrn,
                   wts["w_mix_out"], x)

    x, h = _xattn_block(x, wts["g_xattn"], wts["w_xq"], mem_k, mem_v, wts["w_xo"], wts["g_ffn"], bsz, t)

    hmid, new_ffn = _ffn_up(h, wts["w_up"], wts["w_ffn_dw"], wts["b_ffn_dw"], ffn_st, bsz, t, d_ff)
    x = _matmul(hmid, wts["w_down"], res=x, tm=512, tn=512, name="ffn_down")
    return x, new_conv, new_hgrn, new_ffn


def kernel(x_prompt, x_sample, mem_prompt, state_conv, state_hgrn, cache_mem_k, cache_mem_v, state_ffn_conv, g_mix, w_in, w_conv_dw, b_conv_dw, ln_conv_g, ln_conv_b, w_conv_out, lb_logits, g_hgrn_norm, w_hgrn_out, w_mix_out, g_xattn, g_mem, w_xq, w_xk, w_xv, w_xo, g_ffn, w_up, w_ffn_dw, b_ffn_dw, w_down, g_final):
    depth = w_in.shape[0]
    assert depth == 1 and lb_logits.shape[0] == depth + 1
    bp, tp, d_model = x_prompt.shape
    bs, ts, _ = x_sample.shape
    n_mem = mem_prompt.shape[1]
    d_x = w_xq.shape[2]
    nhx = d_x // XATTN_HEAD_DIM
    d_conv = w_conv_out.shape[1]
    d_hgrn = w_hgrn_out.shape[1]
    d_ff = w_down.shape[1]
    dk = HGRN_HEAD_DIM
    nh = d_hgrn // dk
    cw = w_conv_dw.shape[1]

    def w2(w):
        return w.reshape(w.shape[1:]).astype(BF16)

    def v1(p):
        return p.reshape(p.shape[1:])

    wts = dict(
        g_mix=v1(g_mix), w_in=v1(w_in), w_conv_dw=v1(w_conv_dw), b_conv_dw=v1(b_conv_dw),
        ln_conv_g=v1(ln_conv_g), ln_conv_b=v1(ln_conv_b), w_conv_out=w2(w_conv_out),
        g_hgrn_norm=v1(g_hgrn_norm), w_hgrn_out=w2(w_hgrn_out), w_mix_out=w2(w_mix_out),
        g_xattn=v1(g_xattn), w_xq=w2(w_xq), w_xo=w2(w_xo), g_ffn=v1(g_ffn), w_up=v1(w_up),
        w_ffn_dw=v1(w_ffn_dw), b_ffn_dw=v1(b_ffn_dw), w_down=w2(w_down))

    mem_h = _rmsnorm(mem_prompt.reshape(bp * n_mem, d_model), v1(g_mem), BF16)
    mk = _matmul(mem_h, w2(w_xk), name="mem_k")
    mv = _matmul(mem_h, w2(w_xv), name="mem_v")
    xp, p_conv, p_hgrn, p_ffn = _layer(
        x_prompt.reshape(bp * tp, d_model), mk.reshape(bp, n_mem, d_x), mv.reshape(bp, n_mem, d_x),
        jnp.zeros((bp, cw - 1, d_conv), F32), jnp.zeros((bp, nh, dk, dk), F32),
        jnp.zeros((bp, 2, d_ff), F32), lb_logits, wts, bp, tp)
    xs, s_conv, s_hgrn, s_ffn = _layer(
        x_sample.reshape(bs * ts, d_model), cache_mem_k.reshape(bs, n_mem * nhx, XATTN_HEAD_DIM),
        cache_mem_v.reshape(bs, n_mem * nhx, XATTN_HEAD_DIM), state_conv.reshape(bs, cw - 1, d_conv),
        state_hgrn.reshape(bs, nh, dk, dk), state_ffn_conv.reshape(bs, 2, d_ff),
        lb_logits, wts, bs, ts)

    y_prompt = _rmsnorm(xp, g_final, F32).reshape(bp, tp, d_model)
    y_sample = _rmsnorm(xs, g_final, F32).reshape(bs, ts, d_model)
    return (y_prompt, y_sample,
            p_conv[None], p_hgrn[None],
            mk.reshape(1, bp, n_mem, nhx, XATTN_HEAD_DIM), mv.reshape(1, bp, n_mem, nhx, XATTN_HEAD_DIM),
            p_ffn[None],
            s_conv[None], s_hgrn[None], s_ffn[None])
```
